```python
import jax, jax.numpy as jnp
from jax import lax
import numpy as np

D_MODEL = 1024
BATCH = 8
SEQ = 2048
DEPTH = 2
DEC_BATCH = 128
DEC_SEQ = 1
PAST_LEN = 16384
PAGE_SIZE = 128

D_POOL = D_MODEL // 2
POOL_WINDOWS = (2, 4, 8, 16)
POOL_GROUPS = len(POOL_WINDOWS)
POOL_GW = D_POOL // POOL_GROUPS
POOL_STATE = max(POOL_WINDOWS) - 1
D_RNN = D_MODEL
RNN_HEADS = 8
RNN_HD = D_RNN // RNN_HEADS
CONV_WIDTH = 4
LRU_C = 8.0
D_CHUNK = D_MODEL // 2
CHUNK = 128
CHUNK_GROUPS = 4
CHUNK_GW = D_CHUNK // CHUNK_GROUPS
N_BRANCH = 3
D_FF = 3 * D_MODEL
FFN_CONV = 3
EPS = 1e-6
IN_COLS = D_POOL + 2 * D_RNN + 2 * D_CHUNK + N_BRANCH * D_MODEL
IN_SPLITS = (D_POOL, D_POOL + D_RNN, D_POOL + 2 * D_RNN, D_POOL + 2 * D_RNN + 2 * D_CHUNK)

kernel_name = 'hybrid_pool_rglru_chunkmlp_decode_step'


def rmsnorm(x, g):
    xf = x.astype(jnp.float32)
    y = xf * lax.rsqrt(jnp.mean(xf * xf, axis=-1, keepdims=True) + EPS)
    return (y * g.astype(jnp.float32)).astype(x.dtype)


def causal_dwconv(x_ext, w, b):
    c = x_ext.shape[-1]
    y = lax.conv_general_dilated(x_ext, w[:, None, :].astype(x_ext.dtype), window_strides=(1,),
                                 padding='VALID', dimension_numbers=('NWC', 'WIO', 'NWC'),
                                 feature_group_count=c)
    return y + b


def multiscale_pool(a_ext, pos, pool_w, pool_scale):
    bn, length, _ = a_ext.shape
    t = length - POOL_STATE
    af = a_ext.astype(jnp.float32)
    cs = jnp.concatenate([jnp.zeros((bn, 1, D_POOL), jnp.float32), jnp.cumsum(af, axis=1)], axis=1)
    hi = cs[:, POOL_STATE + 1:]
    lo = jnp.concatenate([cs[:, POOL_STATE + 1 - w: POOL_STATE + 1 - w + t, g * POOL_GW:(g + 1) * POOL_GW]
                          for g, w in enumerate(POOL_WINDOWS)], axis=-1)
    win = jnp.repeat(jnp.array(POOL_WINDOWS, jnp.int32), POOL_GW)
    cnt = jnp.minimum(pos[:, None] + 1, win[None, :]).astype(jnp.float32)
    d = (hi - lo) / cnt - af[:, POOL_STATE:]
    d = d.reshape(bn, t, POOL_GROUPS, POOL_GW)
    y = jnp.einsum('btgi,gij->btgj', d, pool_w.astype(jnp.float32)).reshape(bn, t, D_POOL)
    return (y * pool_scale.astype(jnp.float32)).astype(a_ext.dtype)


def block_diag(x, w, b):
    xh = x.reshape(*x.shape[:-1], RNN_HEADS, RNN_HD)
    return jnp.einsum('bthi,hij->bthj', xh, w).reshape(x.shape) + b


def _lin_combine(left, right):
    a_l, b_l = left
    a_r, b_r = right
    return a_l * a_r, a_r * b_l + b_r


def rg_lru(xc, h0, wa, ba, wx, bx, lam):
    r = jax.nn.sigmoid(block_diag(xc, wa, ba).astype(jnp.float32))
    i = jax.nn.sigmoid(block_diag(xc, wx, bx).astype(jnp.float32))
    log_a = -LRU_C * r * jax.nn.softplus(-lam.astype(jnp.float32))
    a = jnp.exp(log_a)
    b = jnp.sqrt(-jnp.expm1(2.0 * log_a)) * (i * xc.astype(jnp.float32))
    b = b.at[:, 0].add(a[:, 0] * h0.astype(jnp.float32))
    _, h = lax.associative_scan(_lin_combine, (a, b), axis=1)
    return h.astype(xc.dtype), h[:, -1].astype(xc.dtype)


def chunk_mlp(uv, vnorm_g, ws, bs):
    u, v = jnp.split(uv, 2, axis=-1)
    v = rmsnorm(v, vnorm_g)
    bn, t, _ = v.shape
    n_chunks = -(-t // CHUNK)
    tp = n_chunks * CHUNK
    vp = jnp.pad(v, ((0, 0), (0, tp - t), (0, 0))).reshape(bn, n_chunks, CHUNK, CHUNK_GROUPS, CHUNK_GW)
    mask = jnp.tril(jnp.ones((CHUNK, CHUNK), dtype=bool))
    ws_c = jnp.where(mask[None], ws, 0)
    mix = jnp.einsum('gij,bcjgd->bcigd', ws_c, vp) + jnp.swapaxes(bs, 0, 1)[:, :, None]
    mix = mix.reshape(bn, tp, D_CHUNK)[:, :t]
    return u * mix, v


def conv_ffn(xn, prefix, wg, wu, cw, cb, wd):
    g_pre = xn @ wg
    ext = jnp.concatenate([prefix.astype(g_pre.dtype), g_pre], axis=1)
    h = jax.nn.gelu(causal_dwconv(ext, cw, cb)) * (xn @ wu)
    return h @ wd, ext[:, -(FFN_CONV - 1):]


def decoder_layer(x, pos, pool_prefix, rconv_prefix, h0, ffn_prefix, lp):
    xn = rmsnorm(x, lp['norm1_g'])
    z = xn @ lp['w_in']
    a_in, b_x, b_gate, c_uv, gates = jnp.split(z, IN_SPLITS, axis=-1)
    a_ext = jnp.concatenate([pool_prefix.astype(a_in.dtype), a_in], axis=1)
    ya = multiscale_pool(a_ext, pos, lp['pool_w'], lp['pool_scale'])
    b_ext = jnp.concatenate([rconv_prefix.astype(b_x.dtype), b_x], axis=1)
    bc = causal_dwconv(b_ext, lp['rnn_conv_w'], lp['rnn_conv_b'])
    h, h_last = rg_lru(bc, h0, lp['lru_wa'], lp['lru_ba'], lp['lru_wx'], lp['lru_bx'], lp['lru_lambda'])
    yb = jax.nn.gelu(b_gate) * h
    yc, v_rows = chunk_mlp(jax.nn.gelu(c_uv), lp['chunk_vnorm_g'], lp['chunk_ws'], lp['chunk_bs'])
    ga, gb, gc = jnp.split(jax.nn.sigmoid(gates), N_BRANCH, axis=-1)
    merged = ga * (ya @ lp['w_pa']) + gb * (yb @ lp['w_pb']) + gc * (yc @ lp['w_pc'])
    x = x + merged @ lp['w_o']
    f, ffn_state = conv_ffn(rmsnorm(x, lp['norm2_g']), ffn_prefix, lp['ffn_wg'], lp['ffn_wu'],
                            lp['ffn_conv_w'], lp['ffn_conv_b'], lp['ffn_wd'])
    x = x + f
    return x, a_ext[:, -POOL_STATE:], b_ext[:, -(CONV_WIDTH - 1):], h_last, ffn_state, v_rows


def setup_inputs(seed: int = 0) -> dict:
    key = jax.random.key(seed)
    ks = jax.random.split(key, 40)
    f32 = jnp.float32

    def nrm(k, shape, scale):
        return jax.random.normal(k, shape, f32) * scale

    u = jax.random.uniform(ks[16], (DEPTH, D_RNN), f32, minval=0.9, maxval=0.999)
    s = u ** (1.0 / LRU_C)
    lam = jnp.log(s) - jnp.log1p(-s)
    return {
        'x_prompt': nrm(ks[0], (BATCH, SEQ, D_MODEL), 1.0),
        'x_sample': nrm(ks[1], (DEC_BATCH, DEC_SEQ, D_MODEL), 1.0),
        'state_pool': nrm(ks[2], (DEPTH, DEC_BATCH, POOL_STATE, D_POOL), 1.0),
        'state_rnn_conv': nrm(ks[3], (DEPTH, DEC_BATCH, CONV_WIDTH - 1, D_RNN), 1.0),
        'state_rnn_h': nrm(ks[4], (DEPTH, DEC_BATCH, D_RNN), 0.5),
        'state_ffn_conv': nrm(ks[5], (DEPTH, DEC_BATCH, FFN_CONV - 1, D_FF), 1.0),
        'norm1_g': 1.0 + nrm(ks[6], (DEPTH, D_MODEL), 0.05),
        'w_in': nrm(ks[7], (DEPTH, D_MODEL, IN_COLS), D_MODEL ** -0.5),
        'pool_w': nrm(ks[8], (DEPTH, POOL_GROUPS, POOL_GW, POOL_GW), POOL_GW ** -0.5),
        'pool_scale': 1.0 + nrm(ks[9], (DEPTH, D_POOL), 0.05),
        'rnn_conv_w': nrm(ks[10], (DEPTH, CONV_WIDTH, D_RNN), CONV_WIDTH ** -0.5),
        'rnn_conv_b': nrm(ks[11], (DEPTH, D_RNN), 0.01),
        'lru_wa': nrm(ks[12], (DEPTH, RNN_HEADS, RNN_HD, RNN_HD), RNN_HD ** -0.5),
        'lru_ba': nrm(ks[13], (DEPTH, D_RNN), 0.01),
        'lru_wx': nrm(ks[14], (DEPTH, RNN_HEADS, RNN_HD, RNN_HD), RNN_HD ** -0.5),
        'lru_bx': nrm(ks[15], (DEPTH, D_RNN), 0.01),
        'lru_lambda': lam,
        'chunk_vnorm_g': 1.0 + nrm(ks[17], (DEPTH, D_CHUNK), 0.05),
        'chunk_ws': nrm(ks[18], (DEPTH, CHUNK_GROUPS, CHUNK, CHUNK), CHUNK ** -0.5),
        'chunk_bs': 1.0 + nrm(ks[19], (DEPTH, CHUNK_GROUPS, CHUNK), 0.01),
        'w_pa': nrm(ks[20], (DEPTH, D_POOL, D_MODEL), D_POOL ** -0.5),
        'w_pb': nrm(ks[21], (DEPTH, D_RNN, D_MODEL), D_RNN ** -0.5),
        'w_pc': nrm(ks[22], (DEPTH, D_CHUNK, D_MODEL), D_CHUNK ** -0.5),
        'w_o': nrm(ks[23], (DEPTH, D_MODEL, D_MODEL), D_MODEL ** -0.5),
        'norm2_g': 1.0 + nrm(ks[24], (DEPTH, D_MODEL), 0.05),
        'ffn_wg': nrm(ks[25], (DEPTH, D_MODEL, D_FF), D_MODEL ** -0.5),
        'ffn_wu': nrm(ks[26], (DEPTH, D_MODEL, D_FF), D_MODEL ** -0.5),
        'ffn_conv_w': nrm(ks[27], (DEPTH, FFN_CONV, D_FF), FFN_CONV ** -0.5),
        'ffn_conv_b': nrm(ks[28], (DEPTH, D_FF), 0.01),
        'ffn_wd': nrm(ks[29], (DEPTH, D_FF, D_MODEL), D_FF ** -0.5),
        'final_norm_g': 1.0 + nrm(ks[30], (D_MODEL,), 0.05),
    }


def reference(x_prompt, x_sample, state_pool, state_rnn_conv, state_rnn_h, state_ffn_conv,
              norm1_g, w_in, pool_w, pool_scale, rnn_conv_w, rnn_conv_b, lru_wa, lru_ba, lru_wx, lru_bx,
              lru_lambda, chunk_vnorm_g, chunk_ws, chunk_bs, w_pa, w_pb, w_pc, w_o, norm2_g,
              ffn_wg, ffn_wu, ffn_conv_w, ffn_conv_b, ffn_wd, final_norm_g):
    bp, tp = x_prompt.shape[0], x_prompt.shape[1]
    ts = x_sample.shape[1]
    pos_p = jnp.arange(tp, dtype=jnp.int32)
    pos_s = PAST_LEN + jnp.arange(ts, dtype=jnp.int32)
    dt = x_prompt.dtype
    xp, xs = x_prompt, x_sample
    pool_p, pool_s, rc_p, rc_s, h_p, h_s, ff_p, ff_s, cv_s = [], [], [], [], [], [], [], [], []
    for l in range(DEPTH):
        lp = {'norm1_g': norm1_g[l], 'w_in': w_in[l], 'pool_w': pool_w[l], 'pool_scale': pool_scale[l],
              'rnn_conv_w': rnn_conv_w[l], 'rnn_conv_b': rnn_conv_b[l], 'lru_wa': lru_wa[l],
              'lru_ba': lru_ba[l], 'lru_wx': lru_wx[l], 'lru_bx': lru_bx[l], 'lru_lambda': lru_lambda[l],
              'chunk_vnorm_g': chunk_vnorm_g[l], 'chunk_ws': chunk_ws[l], 'chunk_bs': chunk_bs[l],
              'w_pa': w_pa[l], 'w_pb': w_pb[l], 'w_pc': w_pc[l], 'w_o': w_o[l], 'norm2_g': norm2_g[l],
              'ffn_wg': ffn_wg[l], 'ffn_wu': ffn_wu[l], 'ffn_conv_w': ffn_conv_w[l],
              'ffn_conv_b': ffn_conv_b[l], 'ffn_wd': ffn_wd[l]}
        xp, a1, b1, c1, d1, _ = decoder_layer(
            xp, pos_p, jnp.zeros((bp, POOL_STATE, D_POOL), dt), jnp.zeros((bp, CONV_WIDTH - 1, D_RNN), dt),
            jnp.zeros((bp, D_RNN), dt), jnp.zeros((bp, FFN_CONV - 1, D_FF), dt), lp)
        xs, a2, b2, c2, d2, v2 = decoder_layer(
            xs, pos_s, state_pool[l], state_rnn_conv[l], state_rnn_h[l], state_ffn_conv[l], lp)
        pool_p.append(a1); pool_s.append(a2)
        rc_p.append(b1); rc_s.append(b2)
        h_p.append(c1); h_s.append(c2)
        ff_p.append(d1); ff_s.append(d2)
        cv_s.append(v2)
    y_prompt = rmsnorm(xp, final_norm_g)
    y_sample = rmsnorm(xs, final_norm_g)
    return (y_prompt, y_sample,
            jnp.stack(pool_p), jnp.stack(pool_s),
            jnp.stack(rc_p), jnp.stack(rc_s),
            jnp.stack(h_p), jnp.stack(h_s),
            jnp.stack(ff_p), jnp.stack(ff_s),
            jnp.stack(cv_s))
```

```python
import functools
import math

import jax
import jax.numpy as jnp
from jax import lax
from jax.experimental import pallas as pl
from jax.experimental.pallas import tpu as pltpu

D_MODEL = 1024
D_POOL = 512
POOL_WINDOWS = (2, 4, 8, 16)
POOL_GW = 128
POOL_STATE = 15
D_RNN = 1024
RNN_HEADS = 8
RNN_HD = 128
CONV_WIDTH = 4
LRU_C = 8.0
D_CHUNK = 512
CHUNK = 128
CHUNK_GROUPS = 4
CHUNK_GW = 128
D_FF = 3072
FFN_CONV = 3
EPS = 1e-6
PAST_LEN = 16384

C_POOL = 0
C_BX = D_POOL
C_BG = C_BX + D_RNN
C_CU = C_BG + D_RNN
C_CV = C_CU + D_CHUNK
C_GA = C_CV + D_CHUNK
C_GB = C_GA + D_MODEL
C_GC = C_GB + D_MODEL

PB = 8
TILE_T = CHUNK
TILE_R = TILE_T * PB
SUB_T = 32
SUB_R = SUB_T * PB
FFN_R = 512
FFN_C = 512
VMEM_LIMIT = 60 * 1024 * 1024

_GELU_K = math.sqrt(2.0 / math.pi)


def _dot(a, b):
    return jnp.dot(a, b, preferred_element_type=jnp.float32)


def _bf(x):
    return x.astype(jnp.bfloat16)


def _gelu(x):
    return 0.5 * x * (1.0 + jnp.tanh(_GELU_K * (x + 0.044715 * (x * x * x))))


def _sigmoid(x):
    return 0.5 * (jnp.tanh(0.5 * x) + 1.0)


def _rms(x, g):
    return x * lax.rsqrt(jnp.mean(x * x, axis=-1, keepdims=True) + EPS) * g


def _lru_log_decay_scale(lam):
    z = -lam
    sp = jnp.maximum(z, 0.0) + jnp.log1p(jnp.exp(-jnp.abs(z)))
    return -LRU_C * sp


def _lru_coeffs(bc, lru_w_ref, lru_b_ref, dscale):
    a_parts, b_parts = [], []
    for h in range(RNN_HEADS):
        cs = slice(h * RNN_HD, (h + 1) * RNN_HD)
        bch = bc[:, cs]
        ri = _dot(_bf(bch), lru_w_ref[h])
        r = _sigmoid(ri[:, :RNN_HD] + lru_b_ref[0:1, cs])
        i = _sigmoid(ri[:, RNN_HD:] + lru_b_ref[1:2, cs])
        log_a = dscale[:, cs] * r
        a = jnp.exp(log_a)
        a_parts.append(a)
        b_parts.append(jnp.sqrt((1.0 - a) * (1.0 + a)) * (i * bch))
    return jnp.concatenate(a_parts, axis=1), jnp.concatenate(b_parts, axis=1)


def _pool_project(d_parts, pool_w_ref, pool_scale_ref, w_pa_ref):
    ya = [_dot(_bf(d), pool_w_ref[g]) * pool_scale_ref[:, g * POOL_GW:(g + 1) * POOL_GW]
          for g, d in enumerate(d_parts)]
    return _dot(_bf(jnp.concatenate(ya, axis=1)), w_pa_ref[...])


def _prompt_mixer_kernel(x_ref, g1_ref, w_in_ref, pool_w_ref, pool_scale_ref, cw_ref, cb_ref,
                         lru_w_ref, lru_b_ref, lam_ref, vg_ref, kron_ref, bs_ref,
                         w_pa_ref, w_pb_ref, w_pc_ref, w_o_ref,
                         xo_ref, pool_o_ref, rconv_o_ref, h_o_ref,
                         pool_prev, conv_prev, h_carry, a_scr, b_scr, v_scr):
    t = pl.program_id(0)

    @pl.when(t == 0)
    def _():
        pool_prev[...] = jnp.zeros_like(pool_prev)
        conv_prev[...] = jnp.zeros_like(conv_prev)
        h_carry[...] = jnp.zeros_like(h_carry)

    dscale = _lru_log_decay_scale(lam_ref[...])
    g1 = g1_ref[...]

    for k in range(TILE_T // SUB_T):
        r0 = k * SUB_R
        x = x_ref[r0:r0 + SUB_R, :]
        xn = _bf(_rms(x, g1))

        a_in = _dot(xn, w_in_ref[:, C_POOL:C_POOL + D_POOL])
        a_ext = jnp.concatenate([pool_prev[...], a_in], axis=0)
        npre = pool_prev.shape[0]
        pos = t * TILE_T + k * SUB_T + lax.shift_right_logical(
            lax.broadcasted_iota(jnp.int32, (SUB_R, POOL_GW), 0), PB.bit_length() - 1)
        d_parts = []
        for g, w in enumerate(POOL_WINDOWS):
            cs = slice(g * POOL_GW, (g + 1) * POOL_GW)
            s = a_ext[npre - (w - 1) * PB:, cs]
            sh = 1
            while sh < w:
                s = s[sh * PB:] + s[:-sh * PB]
                sh *= 2
            cnt = jnp.minimum(pos + 1, w).astype(jnp.float32)
            d_parts.append(s / cnt - a_in[:, cs])
        pool_prev[...] = a_ext[SUB_R:]
        pa = _pool_project(d_parts, pool_w_ref, pool_scale_ref, w_pa_ref)

        b_x = _dot(xn, w_in_ref[:, C_BX:C_BX + D_RNN])
        b_ext = jnp.concatenate([conv_prev[...], b_x], axis=0)
        bc = cb_ref[...] + cw_ref[CONV_WIDTH - 1:CONV_WIDTH, :] * b_x
        for m in range(1, CONV_WIDTH):
            lo = (CONV_WIDTH - 1 - m) * PB
            bc = bc + cw_ref[CONV_WIDTH - 1 - m:CONV_WIDTH - m, :] * b_ext[lo:lo + SUB_R]
        conv_prev[...] = b_ext[SUB_R:]
        a_dec, b_inp = _lru_coeffs(bc, lru_w_ref, lru_b_ref, dscale)
        a_scr[...] = a_dec
        b_scr[...] = b_inp

        def scan_step(j, h):
            rows = pl.ds(pl.multiple_of(j * PB, PB), PB)
            h = a_scr[rows, :] * h + b_scr[rows, :]
            b_scr[rows, :] = h
            return h

        h_carry[...] = lax.fori_loop(0, SUB_T, scan_step, h_carry[...], unroll=True)
        b_gate = _dot(xn, w_in_ref[:, C_BG:C_BG + D_RNN])
        pb = _dot(_bf(_gelu(b_gate) * b_scr[...]), w_pb_ref[...])

        u = _gelu(_dot(xn, w_in_ref[:, C_CU:C_CU + D_CHUNK]))
        v = _rms(_gelu(_dot(xn, w_in_ref[:, C_CV:C_CV + D_CHUNK])), vg_ref[...])
        v_scr[r0:r0 + SUB_R, :] = _bf(v)
        mix = []
        for g in range(CHUNK_GROUPS):
            cs = slice(g * CHUNK_GW, (g + 1) * CHUNK_GW)
            m = _dot(kron_ref[g, r0:r0 + SUB_R, 0:r0 + SUB_R], v_scr[0:r0 + SUB_R, cs])
            mix.append(m + bs_ref[r0:r0 + SUB_R, g:g + 1])
        pc = _dot(_bf(u * jnp.concatenate(mix, axis=1)), w_pc_ref[...])

        ga = _sigmoid(_dot(xn, w_in_ref[:, C_GA:C_GA + D_MODEL]))
        merged = ga * pa
        gb = _sigmoid(_dot(xn, w_in_ref[:, C_GB:C_GB + D_MODEL]))
        merged = merged + gb * pb
        gc = _sigmoid(_dot(xn, w_in_ref[:, C_GC:C_GC + D_MODEL]))
        merged = merged + gc * pc
        xo_ref[r0:r0 + SUB_R, :] = x + _dot(_bf(merged), w_o_ref[...])

    pool_o_ref[...] = pool_prev[...]
    rconv_o_ref[...] = conv_prev[...]
    h_o_ref[...] = h_carry[...]


def _resident(shape):
    nd = len(shape)
    return pl.BlockSpec(shape, lambda *_: (0,) * nd, pipeline_mode=pl.Buffered(1))


def _prompt_mixer(x_tm, w):
    n_rows = x_tm.shape[0]
    n_tiles = n_rows // TILE_R
    weights = [w['g1'], w['w_in'], w['pool_w'], w['pool_scale'], w['cw'], w['cb'], w['lru_w'],
               w['lru_b'], w['lam'], w['vg'], w['kron'], w['bs_tm'], w['w_pa'], w['w_pb'],
               w['w_pc'], w['w_o']]
    row_spec = pl.BlockSpec((TILE_R, D_MODEL), lambda t: (t, 0))
    pool_rows = (POOL_STATE + 1) * PB
    conv_rows = (CONV_WIDTH - 1) * PB
    return pl.pallas_call(
        _prompt_mixer_kernel,
        grid=(n_tiles,),
        in_specs=[row_spec] + [_resident(a.shape) for a in weights],
        out_specs=[row_spec,
                   pl.BlockSpec((pool_rows, D_POOL), lambda t: (0, 0)),
                   pl.BlockSpec((conv_rows, D_RNN), lambda t: (0, 0)),
                   pl.BlockSpec((PB, D_RNN), lambda t: (0, 0))],
        out_shape=[jax.ShapeDtypeStruct((n_rows, D_MODEL), jnp.float32),
                   jax.ShapeDtypeStruct((pool_rows, D_POOL), jnp.float32),
                   jax.ShapeDtypeStruct((conv_rows, D_RNN), jnp.float32),
                   jax.ShapeDtypeStruct((PB, D_RNN), jnp.float32)],
        scratch_shapes=[pltpu.VMEM((pool_rows, D_POOL), jnp.float32),
                        pltpu.VMEM((conv_rows, D_RNN), jnp.float32),
                        pltpu.VMEM((PB, D_RNN), jnp.float32),
                        pltpu.VMEM((SUB_R, D_RNN), jnp.float32),
                        pltpu.VMEM((SUB_R, D_RNN), jnp.float32),
                        pltpu.VMEM((TILE_R, D_CHUNK), jnp.bfloat16)],
        compiler_params=pltpu.CompilerParams(dimension_semantics=("arbitrary",),
                                             vmem_limit_bytes=VMEM_LIMIT),
        name="prompt_mixer",
    )(x_tm, *weights)


def _prompt_ffn_kernel(x_ref, g2_ref, wg_ref, wu_ref, fcw_ref, fcb_ref, wd_ref, gf_ref,
                       xo_ref, ffn_o_ref, g_prev, *, final_norm):
    t = pl.program_id(0)

    @pl.when(t == 0)
    def _():
        g_prev[...] = jnp.zeros_like(g_prev)

    x = x_ref[...]
    xn = _bf(_rms(x, g2_ref[...]))
    acc = x
    npre = (FFN_CONV - 1) * PB
    for c in range(D_FF // FFN_C):
        cs = slice(c * FFN_C, (c + 1) * FFN_C)
        g_pre = _dot(xn, wg_ref[:, cs])
        g_ext = jnp.concatenate([g_prev[:, cs], g_pre], axis=0)
        y = fcb_ref[:, cs] + fcw_ref[FFN_CONV - 1:FFN_CONV, cs] * g_pre
        for m in range(1, FFN_CONV):
            lo = npre - m * PB
            y = y + fcw_ref[FFN_CONV - 1 - m:FFN_CONV - m, cs] * g_ext[lo:lo + FFN_R]
        g_prev[:, cs] = g_ext[FFN_R:]
        h = _gelu(y) * _dot(xn, wu_ref[:, cs])
        acc = acc + _dot(_bf(h), wd_ref[cs, :])
    if final_norm:
        acc = _rms(acc, gf_ref[...])
    xo_ref[...] = acc
    ffn_o_ref[...] = g_prev[...]


def _prompt_ffn(x_tm, w, final_norm):
    n_rows = x_tm.shape[0]
    weights = [w['g2'], w['wg'], w['wu'], w['fcw'], w['fcb'], w['wd'], w['gf']]
    row_spec = pl.BlockSpec((FFN_R, D_MODEL), lambda t: (t, 0))
    pre_rows = (FFN_CONV - 1) * PB
    return pl.pallas_call(
        functools.partial(_prompt_ffn_kernel, final_norm=final_norm),
        grid=(n_rows // FFN_R,),
        in_specs=[row_spec] + [_resident(a.shape) for a in weights],
        out_specs=[row_spec, pl.BlockSpec((pre_rows, D_FF), lambda t: (0, 0))],
        out_shape=[jax.ShapeDtypeStruct((n_rows, D_MODEL), jnp.float32),
                   jax.ShapeDtypeStruct((pre_rows, D_FF), jnp.float32)],
        scratch_shapes=[pltpu.VMEM((pre_rows, D_FF), jnp.float32)],
        compiler_params=pltpu.CompilerParams(dimension_semantics=("arbitrary",),
                                             vmem_limit_bytes=VMEM_LIMIT),
        name="prompt_ffn",
    )(x_tm, *weights)


def _sample_mixer_kernel(x_ref, pool_st_ref, rconv_st_ref, h0_ref,
                         g1_ref, w_in_ref, pool_w_ref, pool_scale_ref, cw_ref, cb_ref,
                         lru_w_ref, lru_b_ref, lam_ref, vg_ref, ws0_ref, bs0_ref,
                         w_pa_ref, w_pb_ref, w_pc_ref, w_o_ref,
                         xo_ref, pool_o_ref, rconv_o_ref, h_o_ref, v_o_ref):
    x = x_ref[...]
    xn = _bf(_rms(x, g1_ref[...]))

    a_in = _dot(xn, w_in_ref[:, C_POOL:C_POOL + D_POOL])
    d_parts = []
    for g, w in enumerate(POOL_WINDOWS):
        s = a_in[:, g * POOL_GW:(g + 1) * POOL_GW]
        for m in range(1, w):
            c0 = (POOL_STATE - m) * D_POOL + g * POOL_GW
            s = s + pool_st_ref[:, c0:c0 + POOL_GW]
        cnt = float(min(PAST_LEN + 1, w))
        d_parts.append(s / cnt - a_in[:, g * POOL_GW:(g + 1) * POOL_GW])
    pool_o_ref[:, :(POOL_STATE - 1) * D_POOL] = pool_st_ref[:, D_POOL:]
    pool_o_ref[:, (POOL_STATE - 1) * D_POOL:] = a_in
    pa = _pool_project(d_parts, pool_w_ref, pool_scale_ref, w_pa_ref)

    b_x = _dot(xn, w_in_ref[:, C_BX:C_BX + D_RNN])
    bc = cb_ref[...] + cw_ref[CONV_WIDTH - 1:CONV_WIDTH, :] * b_x
    for m in range(CONV_WIDTH - 1):
        bc = bc + cw_ref[m:m + 1, :] * rconv_st_ref[:, m * D_RNN:(m + 1) * D_RNN]
    rconv_o_ref[:, :(CONV_WIDTH - 2) * D_RNN] = rconv_st_ref[:, D_RNN:]
    rconv_o_ref[:, (CONV_WIDTH - 2) * D_RNN:] = b_x
    a_dec, b_inp = _lru_coeffs(bc, lru_w_ref, lru_b_ref, _lru_log_decay_scale(lam_ref[...]))
    h = a_dec * h0_ref[...] + b_inp
    h_o_ref[...] = h
    b_gate = _dot(xn, w_in_ref[:, C_BG:C_BG + D_RNN])
    pb = _dot(_bf(_gelu(b_gate) * h), w_pb_ref[...])

    u = _gelu(_dot(xn, w_in_ref[:, C_CU:C_CU + D_CHUNK]))
    v = _rms(_gelu(_dot(xn, w_in_ref[:, C_CV:C_CV + D_CHUNK])), vg_ref[...])
    v_o_ref[...] = v
    mix = ws0_ref[...] * v + bs0_ref[...]
    pc = _dot(_bf(u * mix), w_pc_ref[...])

    merged = _sigmoid(_dot(xn, w_in_ref[:, C_GA:C_GA + D_MODEL])) * pa
    merged = merged + _sigmoid(_dot(xn, w_in_ref[:, C_GB:C_GB + D_MODEL])) * pb
    merged = merged + _sigmoid(_dot(xn, w_in_ref[:, C_GC:C_GC + D_MODEL])) * pc
    xo_ref[...] = x + _dot(_bf(merged), w_o_ref[...])


def _sample_mixer(xs, pool_st, rconv_st, h0, w):
    n = xs.shape[0]
    f32 = jnp.float32
    return pl.pallas_call(
        _sample_mixer_kernel,
        out_shape=[jax.ShapeDtypeStruct((n, D_MODEL), f32),
                   jax.ShapeDtypeStruct((n, POOL_STATE * D_POOL), f32),
                   jax.ShapeDtypeStruct((n, (CONV_WIDTH - 1) * D_RNN), f32),
                   jax.ShapeDtypeStruct((n, D_RNN), f32),
                   jax.ShapeDtypeStruct((n, D_CHUNK), f32)],
        compiler_params=pltpu.CompilerParams(vmem_limit_bytes=VMEM_LIMIT),
        name="sample_mixer",
    )(xs, pool_st, rconv_st, h0, w['g1'], w['w_in'], w['pool_w'], w['pool_scale'], w['cw'],
      w['cb'], w['lru_w'], w['lru_b'], w['lam'], w['vg'], w['ws0'], w['bs0'],
      w['w_pa'], w['w_pb'], w['w_pc'], w['w_o'])


def _sample_ffn_kernel(x_ref, st_ref, g2_ref, wg_ref, wu_ref, fcw_ref, fcb_ref, wd_ref, gf_ref,
                       xo_ref, st_o_ref, *, final_norm):
    x = x_ref[...]
    xn = _bf(_rms(x, g2_ref[...]))
    g_pre = _dot(xn, wg_ref[...])
    y = fcb_ref[...] + fcw_ref[FFN_CONV - 1:FFN_CONV, :] * g_pre
    for m in range(FFN_CONV - 1):
        y = y + fcw_ref[m:m + 1, :] * st_ref[:, m * D_FF:(m + 1) * D_FF]
    st_o_ref[:, :(FFN_CONV - 2) * D_FF] = st_ref[:, D_FF:]
    st_o_ref[:, (FFN_CONV - 2) * D_FF:] = g_pre
    h = _gelu(y) * _dot(xn, wu_ref[...])
    out = x + _dot(_bf(h), wd_ref[...])
    if final_norm:
        out = _rms(out, gf_ref[...])
    xo_ref[...] = out


def _sample_ffn(xs, st, w, final_norm):
    n = xs.shape[0]
    return pl.pallas_call(
        functools.partial(_sample_ffn_kernel, final_norm=final_norm),
        out_shape=[jax.ShapeDtypeStruct((n, D_MODEL), jnp.float32),
                   jax.ShapeDtypeStruct((n, (FFN_CONV - 1) * D_FF), jnp.float32)],
        compiler_params=pltpu.CompilerParams(vmem_limit_bytes=VMEM_LIMIT),
        name="sample_ffn",
    )(xs, st, w['g2'], w['wg'], w['wu'], w['fcw'], w['fcb'], w['wd'], w['gf'])


def _layer_weights(l, p):
    bf16 = jnp.bfloat16
    row = lambda a: a.reshape(1, -1)
    ws_c = jnp.where(jnp.tril(jnp.ones((CHUNK, CHUNK), dtype=bool))[None], p['chunk_ws'][l], 0)
    eye = jnp.eye(PB, dtype=ws_c.dtype)
    kron = (ws_c[:, :, None, :, None] * eye[None, None, :, None, :]).reshape(
        CHUNK_GROUPS, TILE_R, TILE_R)
    return {
        'g1': row(p['norm1_g'][l]), 'w_in': p['w_in'][l].astype(bf16),
        'pool_w': p['pool_w'][l].astype(bf16), 'pool_scale': row(p['pool_scale'][l]),
        'cw': p['rnn_conv_w'][l], 'cb': row(p['rnn_conv_b'][l]),
        'lru_w': jnp.concatenate([p['lru_wa'][l], p['lru_wx'][l]], axis=-1).astype(bf16),
        'lru_b': jnp.stack([p['lru_ba'][l], p['lru_bx'][l]]),
        'lam': row(p['lru_lambda'][l]), 'vg': row(p['chunk_vnorm_g'][l]),
        'kron': kron.astype(bf16),
        'bs_tm': jnp.repeat(p['chunk_bs'][l].T, PB, axis=0),
        'ws0': row(jnp.repeat(p['chunk_ws'][l][:, 0, 0], CHUNK_GW)),
        'bs0': row(jnp.repeat(p['chunk_bs'][l][:, 0], CHUNK_GW)),
        'w_pa': p['w_pa'][l].astype(bf16), 'w_pb': p['w_pb'][l].astype(bf16),
        'w_pc': p['w_pc'][l].astype(bf16), 'w_o': p['w_o'][l].astype(bf16),
        'g2': row(p['norm2_g'][l]), 'wg': p['ffn_wg'][l].astype(bf16),
        'wu': p['ffn_wu'][l].astype(bf16), 'fcw': p['ffn_conv_w'][l],
        'fcb': row(p['ffn_conv_b'][l]), 'wd': p['ffn_wd'][l].astype(bf16),
        'gf': row(p['final_norm_g']),
    }


def kernel(x_prompt, x_sample, state_pool, state_rnn_conv, state_rnn_h, state_ffn_conv, norm1_g, w_in, pool_w, pool_scale, rnn_conv_w, rnn_conv_b, lru_wa, lru_ba, lru_wx, lru_bx, lru_lambda, chunk_vnorm_g, chunk_ws, chunk_bs, w_pa, w_pb, w_pc, w_o, norm2_g, ffn_wg, ffn_wu, ffn_conv_w, ffn_conv_b, ffn_wd, final_norm_g):
    p = dict(norm1_g=norm1_g, w_in=w_in, pool_w=pool_w, pool_scale=pool_scale,
             rnn_conv_w=rnn_conv_w, rnn_conv_b=rnn_conv_b, lru_wa=lru_wa, lru_ba=lru_ba,
             lru_wx=lru_wx, lru_bx=lru_bx, lru_lambda=lru_lambda, chunk_vnorm_g=chunk_vnorm_g,
             chunk_ws=chunk_ws, chunk_bs=chunk_bs, w_pa=w_pa, w_pb=w_pb, w_pc=w_pc, w_o=w_o,
             norm2_g=norm2_g, ffn_wg=ffn_wg, ffn_wu=ffn_wu, ffn_conv_w=ffn_conv_w,
             ffn_conv_b=ffn_conv_b, ffn_wd=ffn_wd, final_norm_g=final_norm_g)
    depth = w_in.shape[0]
    bp, tp, _ = x_prompt.shape
    ns = x_sample.shape[0]
    assert bp == PB and tp % TILE_T == 0 and x_sample.shape[1] == 1

    xp = jnp.transpose(x_prompt, (1, 0, 2)).reshape(tp * bp, D_MODEL)
    xs = x_sample.reshape(ns, D_MODEL)
    outs = {k: [] for k in ('pool_p', 'pool_s', 'rc_p', 'rc_s', 'h_p', 'h_s', 'ff_p', 'ff_s', 'cv_s')}
    for l in range(depth):
        w = _layer_weights(l, p)
        last = l == depth - 1
        xp, pool_tm, rconv_tm, h_p = _prompt_mixer(xp, w)
        xp, ffn_tm = _prompt_ffn(xp, w, last)
        xs, pool_s, rconv_s, h_s, v_s = _sample_mixer(
            xs, state_pool[l].reshape(ns, -1), state_rnn_conv[l].reshape(ns, -1), state_rnn_h[l], w)
        xs, ffn_s = _sample_ffn(xs, state_ffn_conv[l].reshape(ns, -1), w, last)
        tm = lambda a, n: jnp.transpose(a.reshape(-1, PB, a.shape[-1])[-n:], (1, 0, 2))
        outs['pool_p'].append(tm(pool_tm, POOL_STATE))
        outs['rc_p'].append(tm(rconv_tm, CONV_WIDTH - 1))
        outs['h_p'].append(h_p)
        outs['ff_p'].append(tm(ffn_tm, FFN_CONV - 1))
        outs['pool_s'].append(pool_s.reshape(ns, POOL_STATE, D_POOL))
        outs['rc_s'].append(rconv_s.reshape(ns, CONV_WIDTH - 1, D_RNN))
        outs['h_s'].append(h_s)
        outs['ff_s'].append(ffn_s.reshape(ns, FFN_CONV - 1, D_FF))
        outs['cv_s'].append(v_s.reshape(ns, 1, D_CHUNK))
    y_prompt = jnp.transpose(xp.reshape(tp, bp, D_MODEL), (1, 0, 2))
    y_sample = xs.reshape(ns, 1, D_MODEL)
    st = jnp.stack
    return (y_prompt, y_sample, st(outs['pool_p']), st(outs['pool_s']), st(outs['rc_p']),
            st(outs['rc_s']), st(outs['h_p']), st(outs['h_s']), st(outs['ff_p']),
            st(outs['ff_s']), st(outs['cv_s']))
```

```python
import functools
import math

import jax
import jax.numpy as jnp
from jax import lax
from jax.experimental import pallas as pl
from jax.experimental.pallas import tpu as pltpu

D_MODEL = 1024
D_POOL = 512
POOL_WINDOWS = (2, 4, 8, 16)
POOL_GW = 128
POOL_STATE = 15
D_RNN = 1024
RNN_HEADS = 8
RNN_HD = 128
CONV_WIDTH = 4
LRU_C = 8.0
D_CHUNK = 512
CHUNK = 128
CHUNK_GROUPS = 4
CHUNK_GW = 128
D_FF = 3072
FFN_CONV = 3
EPS = 1e-6
PAST_LEN = 16384

C_POOL = 0
C_BX = D_POOL
C_BG = C_BX + D_RNN
C_CU = C_BG + D_RNN
C_CV = C_CU + D_CHUNK
C_GA = C_CV + D_CHUNK
C_GB = C_GA + D_MODEL
C_GC = C_GB + D_MODEL

PB = 8
PB_SHIFT = PB.bit_length() - 1
TILE_T = CHUNK
TILE_R = TILE_T * PB
SUB_T = 32
SUB_R = SUB_T * PB
FFN_R = 512
FFN_T = FFN_R // PB
FFN_C = 512
VMEM_LIMIT = 60 * 1024 * 1024

_GELU_K = math.sqrt(2.0 / math.pi)


def _dot(a, b):
    return jnp.dot(a, b, preferred_element_type=jnp.float32)


def _bf(x):
    return x.astype(jnp.bfloat16)


def _gelu(x):
    return 0.5 * x * (1.0 + jnp.tanh(_GELU_K * (x + 0.044715 * (x * x * x))))


def _sigmoid(x):
    return 0.5 * (jnp.tanh(0.5 * x) + 1.0)


def _rms(x, g):
    return x * lax.rsqrt(jnp.mean(x * x, axis=-1, keepdims=True) + EPS) * g


def _lru_log_decay_scale(lam):
    z = -lam
    sp = jnp.maximum(z, 0.0) + jnp.log1p(jnp.exp(-jnp.abs(z)))
    return -LRU_C * sp


def _lru_coeffs(bc, lru_w_ref, lru_b_ref, dscale):
    a_parts, b_parts = [], []
    for h in range(RNN_HEADS):
        cs = slice(h * RNN_HD, (h + 1) * RNN_HD)
        bch = bc[:, cs]
        ri = _dot(_bf(bch), lru_w_ref[h])
        r = _sigmoid(ri[:, :RNN_HD] + lru_b_ref[0:1, cs])
        i = _sigmoid(ri[:, RNN_HD:] + lru_b_ref[1:2, cs])
        log_a = dscale[:, cs] * r
        a = jnp.exp(log_a)
        a_parts.append(a)
        b_parts.append(jnp.sqrt((1.0 - a) * (1.0 + a)) * (i * bch))
    return jnp.concatenate(a_parts, axis=1), jnp.concatenate(b_parts, axis=1)


def _pool_project(d_parts, pool_w_ref, pool_scale_ref, w_pa_ref):
    ya = [_dot(_bf(d), pool_w_ref[g]) * pool_scale_ref[:, g * POOL_GW:(g + 1) * POOL_GW]
          for g, d in enumerate(d_parts)]
    return _dot(_bf(jnp.concatenate(ya, axis=1)), w_pa_ref[...])


def _build_time_major_spatial_weights(ws_ref, kron_ref):
    row_t = lax.shift_right_logical(lax.broadcasted_iota(jnp.int32, (SUB_R, SUB_T), 0), PB_SHIFT)
    repeat_rows = _bf(row_t == lax.broadcasted_iota(jnp.int32, (SUB_R, SUB_T), 1))
    col_t = lax.shift_right_logical(lax.broadcasted_iota(jnp.int32, (CHUNK, TILE_R), 1), PB_SHIFT)
    repeat_cols = _bf(col_t == lax.broadcasted_iota(jnp.int32, (CHUNK, TILE_R), 0))
    rr = lax.broadcasted_iota(jnp.int32, (SUB_R, TILE_R), 0)
    cc = lax.broadcasted_iota(jnp.int32, (SUB_R, TILE_R), 1)
    same_seq = (rr & (PB - 1)) == (cc & (PB - 1))
    for k in range(TILE_T // SUB_T):
        causal = (lax.shift_right_logical(rr, PB_SHIFT) + k * SUB_T) >= lax.shift_right_logical(cc, PB_SHIFT)
        keep = same_seq & causal
        for g in range(CHUNK_GROUPS):
            rows = _dot(repeat_rows, ws_ref[g, k * SUB_T:(k + 1) * SUB_T, :])
            full = _dot(_bf(rows), repeat_cols)
            kron_ref[g, k * SUB_R:(k + 1) * SUB_R, :] = _bf(jnp.where(keep, full, 0.0))


def _prompt_mixer_kernel(x_ref, g1_ref, w_in_ref, pool_w_ref, pool_scale_ref, cw_ref, cb_ref,
                         lru_w_ref, lru_b_ref, lam_ref, vg_ref, ws_ref, bs_ref,
                         w_pa_ref, w_pb_ref, w_pc_ref, w_o_ref,
                         xo_ref, pool_o_ref, rconv_o_ref, h_o_ref,
                         pool_prev, conv_prev, h_carry, a_scr, b_scr, v_scr, kron_ref,
                         *, batch_major_in):
    t = pl.program_id(0)

    @pl.when(t == 0)
    def _():
        pool_prev[...] = jnp.zeros_like(pool_prev)
        conv_prev[...] = jnp.zeros_like(conv_prev)
        h_carry[...] = jnp.zeros_like(h_carry)
        _build_time_major_spatial_weights(ws_ref, kron_ref)

    dscale = _lru_log_decay_scale(lam_ref[...])
    g1 = g1_ref[...]

    for k in range(TILE_T // SUB_T):
        r0 = k * SUB_R
        if batch_major_in:
            x = pltpu.einshape("sjc->jsc", x_ref[:, k * SUB_T:(k + 1) * SUB_T, :]).reshape(SUB_R, D_MODEL)
        else:
            x = x_ref[r0:r0 + SUB_R, :]
        xn = _bf(_rms(x, g1))

        a_in = _dot(xn, w_in_ref[:, C_POOL:C_POOL + D_POOL])
        a_ext = jnp.concatenate([pool_prev[...], a_in], axis=0)
        npre = pool_prev.shape[0]
        pos = t * TILE_T + k * SUB_T + lax.shift_right_logical(
            lax.broadcasted_iota(jnp.int32, (SUB_R, POOL_GW), 0), PB_SHIFT)
        d_parts = []
        for g, w in enumerate(POOL_WINDOWS):
            cs = slice(g * POOL_GW, (g + 1) * POOL_GW)
            s = a_ext[npre - (w - 1) * PB:, cs]
            sh = 1
            while sh < w:
                s = s[sh * PB:] + s[:-sh * PB]
                sh *= 2
            cnt = jnp.minimum(pos + 1, w).astype(jnp.float32)
            d_parts.append(s / cnt - a_in[:, cs])
        pool_prev[...] = a_ext[SUB_R:]
        pa = _pool_project(d_parts, pool_w_ref, pool_scale_ref, w_pa_ref)

        b_x = _dot(xn, w_in_ref[:, C_BX:C_BX + D_RNN])
        b_ext = jnp.concatenate([conv_prev[...], b_x], axis=0)
        bc = cb_ref[...] + cw_ref[CONV_WIDTH - 1:CONV_WIDTH, :] * b_x
        for m in range(1, CONV_WIDTH):
            lo = (CONV_WIDTH - 1 - m) * PB
            bc = bc + cw_ref[CONV_WIDTH - 1 - m:CONV_WIDTH - m, :] * b_ext[lo:lo + SUB_R]
        conv_prev[...] = b_ext[SUB_R:]
        a_dec, b_inp = _lru_coeffs(bc, lru_w_ref, lru_b_ref, dscale)
        a_scr[...] = a_dec
        b_scr[...] = b_inp

        def scan_step(j, h):
            rows = pl.ds(pl.multiple_of(j * PB, PB), PB)
            h = a_scr[rows, :] * h + b_scr[rows, :]
            b_scr[rows, :] = h
            return h

        h_carry[...] = lax.fori_loop(0, SUB_T, scan_step, h_carry[...], unroll=True)
        b_gate = _dot(xn, w_in_ref[:, C_BG:C_BG + D_RNN])
        pb = _dot(_bf(_gelu(b_gate) * b_scr[...]), w_pb_ref[...])

        u = _gelu(_dot(xn, w_in_ref[:, C_CU:C_CU + D_CHUNK]))
        v = _rms(_gelu(_dot(xn, w_in_ref[:, C_CV:C_CV + D_CHUNK])), vg_ref[...])
        v_scr[r0:r0 + SUB_R, :] = _bf(v)
        mix = []
        for g in range(CHUNK_GROUPS):
            cs = slice(g * CHUNK_GW, (g + 1) * CHUNK_GW)
            m = _dot(kron_ref[g, r0:r0 + SUB_R, 0:r0 + SUB_R], v_scr[0:r0 + SUB_R, cs])
            mix.append(m + bs_ref[r0:r0 + SUB_R, g:g + 1])
        pc = _dot(_bf(u * jnp.concatenate(mix, axis=1)), w_pc_ref[...])

        ga = _sigmoid(_dot(xn, w_in_ref[:, C_GA:C_GA + D_MODEL]))
        merged = ga * pa
        gb = _sigmoid(_dot(xn, w_in_ref[:, C_GB:C_GB + D_MODEL]))
        merged = merged + gb * pb
        gc = _sigmoid(_dot(xn, w_in_ref[:, C_GC:C_GC + D_MODEL]))
        merged = merged + gc * pc
        xo_ref[r0:r0 + SUB_R, :] = x + _dot(_bf(merged), w_o_ref[...])

    pool_o_ref[...] = pool_prev[...]
    rconv_o_ref[...] = conv_prev[...]
    h_o_ref[...] = h_carry[...]


def _layer_spec(a, l):
    nd = a.ndim
    return pl.BlockSpec((None,) + a.shape[1:], lambda *_: (l,) + (0,) * (nd - 1),
                        pipeline_mode=pl.Buffered(1))


def _whole_spec(a):
    nd = a.ndim
    return pl.BlockSpec(a.shape, lambda *_: (0,) * nd, pipeline_mode=pl.Buffered(1))


_MIXER_PARAMS = ('g1', 'w_in', 'pool_w', 'pool_scale', 'cw', 'cb', 'lru_w', 'lru_b', 'lam', 'vg')
_PROJ_PARAMS = ('w_pa', 'w_pb', 'w_pc', 'w_o')
_FFN_PARAMS = ('g2', 'wg', 'wu', 'fcw', 'fcb', 'wd')


def _prompt_mixer(x, w, l, batch_major_in):
    if batch_major_in:
        n_rows = x.shape[0] * x.shape[1]
        x_spec = pl.BlockSpec((PB, TILE_T, D_MODEL), lambda t: (0, t, 0))
    else:
        n_rows = x.shape[0]
        x_spec = pl.BlockSpec((TILE_R, D_MODEL), lambda t: (t, 0))
    weights = [w[k] for k in _MIXER_PARAMS + ('ws', 'bs_tm') + _PROJ_PARAMS]
    pool_rows = (POOL_STATE + 1) * PB
    conv_rows = (CONV_WIDTH - 1) * PB
    f32 = jnp.float32
    return pl.pallas_call(
        functools.partial(_prompt_mixer_kernel, batch_major_in=batch_major_in),
        grid=(n_rows // TILE_R,),
        in_specs=[x_spec] + [_layer_spec(a, l) for a in weights],
        out_specs=[pl.BlockSpec((TILE_R, D_MODEL), lambda t: (t, 0)),
                   pl.BlockSpec((pool_rows, D_POOL), lambda t: (0, 0)),
                   pl.BlockSpec((conv_rows, D_RNN), lambda t: (0, 0)),
                   pl.BlockSpec((PB, D_RNN), lambda t: (0, 0))],
        out_shape=[jax.ShapeDtypeStruct((n_rows, D_MODEL), f32),
                   jax.ShapeDtypeStruct((pool_rows, D_POOL), f32),
                   jax.ShapeDtypeStruct((conv_rows, D_RNN), f32),
                   jax.ShapeDtypeStruct((PB, D_RNN), f32)],
        scratch_shapes=[pltpu.VMEM((pool_rows, D_POOL), f32),
                        pltpu.VMEM((conv_rows, D_RNN), f32),
                        pltpu.VMEM((PB, D_RNN), f32),
                        pltpu.VMEM((SUB_R, D_RNN), f32),
                        pltpu.VMEM((SUB_R, D_RNN), f32),
                        pltpu.VMEM((TILE_R, D_CHUNK), jnp.bfloat16),
                        pltpu.VMEM((CHUNK_GROUPS, TILE_R, TILE_R), jnp.bfloat16)],
        compiler_params=pltpu.CompilerParams(dimension_semantics=("arbitrary",),
                                             vmem_limit_bytes=VMEM_LIMIT),
        name="prompt_mixer",
    )(x, *weights)


def _prompt_ffn_kernel(x_ref, g2_ref, wg_ref, wu_ref, fcw_ref, fcb_ref, wd_ref, gf_ref,
                       xo_ref, ffn_o_ref, g_prev, *, last_layer):
    t = pl.program_id(0)

    @pl.when(t == 0)
    def _():
        g_prev[...] = jnp.zeros_like(g_prev)

    x = x_ref[...]
    xn = _bf(_rms(x, g2_ref[...]))
    acc = x
    npre = (FFN_CONV - 1) * PB
    for c in range(D_FF // FFN_C):
        cs = slice(c * FFN_C, (c + 1) * FFN_C)
        g_pre = _dot(xn, wg_ref[:, cs])
        g_ext = jnp.concatenate([g_prev[:, cs], g_pre], axis=0)
        y = fcb_ref[:, cs] + fcw_ref[FFN_CONV - 1:FFN_CONV, cs] * g_pre
        for m in range(1, FFN_CONV):
            lo = npre - m * PB
            y = y + fcw_ref[FFN_CONV - 1 - m:FFN_CONV - m, cs] * g_ext[lo:lo + FFN_R]
        g_prev[:, cs] = g_ext[FFN_R:]
        h = _gelu(y) * _dot(xn, wu_ref[:, cs])
        acc = acc + _dot(_bf(h), wd_ref[cs, :])
    if last_layer:
        y = _rms(acc, gf_ref[...]).reshape(FFN_T, PB, D_MODEL)
        xo_ref[...] = pltpu.einshape("jsc->sjc", y)
    else:
        xo_ref[...] = acc
    ffn_o_ref[...] = g_prev[...]


def _prompt_ffn(x_tm, w, l, last_layer):
    n_rows = x_tm.shape[0]
    row_spec = pl.BlockSpec((FFN_R, D_MODEL), lambda t: (t, 0))
    pre_rows = (FFN_CONV - 1) * PB
    if last_layer:
        out_spec = pl.BlockSpec((PB, FFN_T, D_MODEL), lambda t: (0, t, 0))
        out_shape = jax.ShapeDtypeStruct((PB, n_rows // PB, D_MODEL), jnp.float32)
    else:
        out_spec, out_shape = row_spec, jax.ShapeDtypeStruct((n_rows, D_MODEL), jnp.float32)
    weights = [w[k] for k in _FFN_PARAMS]
    return pl.pallas_call(
        functools.partial(_prompt_ffn_kernel, last_layer=last_layer),
        grid=(n_rows // FFN_R,),
        in_specs=[row_spec] + [_layer_spec(a, l) for a in weights] + [_whole_spec(w['gf'])],
        out_specs=[out_spec, pl.BlockSpec((pre_rows, D_FF), lambda t: (0, 0))],
        out_shape=[out_shape, jax.ShapeDtypeStruct((pre_rows, D_FF), jnp.float32)],
        scratch_shapes=[pltpu.VMEM((pre_rows, D_FF), jnp.float32)],
        compiler_params=pltpu.CompilerParams(dimension_semantics=("arbitrary",),
                                             vmem_limit_bytes=VMEM_LIMIT),
        name="prompt_ffn",
    )(x_tm, *weights, w['gf'])


def _sample_mixer_kernel(x_ref, pool_st_ref, rconv_st_ref, h0_ref,
                         g1_ref, w_in_ref, pool_w_ref, pool_scale_ref, cw_ref, cb_ref,
                         lru_w_ref, lru_b_ref, lam_ref, vg_ref, ws0_ref, bs0_ref,
                         w_pa_ref, w_pb_ref, w_pc_ref, w_o_ref,
                         xo_ref, pool_o_ref, rconv_o_ref, h_o_ref, v_o_ref):
    x = x_ref[...]
    xn = _bf(_rms(x, g1_ref[...]))

    a_in = _dot(xn, w_in_ref[:, C_POOL:C_POOL + D_POOL])
    d_parts = []
    for g, w in enumerate(POOL_WINDOWS):
        s = a_in[:, g * POOL_GW:(g + 1) * POOL_GW]
        for m in range(1, w):
            c0 = (POOL_STATE - m) * D_POOL + g * POOL_GW
            s = s + pool_st_ref[:, c0:c0 + POOL_GW]
        cnt = float(min(PAST_LEN + 1, w))
        d_parts.append(s / cnt - a_in[:, g * POOL_GW:(g + 1) * POOL_GW])
    pool_o_ref[:, :(POOL_STATE - 1) * D_POOL] = pool_st_ref[:, D_POOL:]
    pool_o_ref[:, (POOL_STATE - 1) * D_POOL:] = a_in
    pa = _pool_project(d_parts, pool_w_ref, pool_scale_ref, w_pa_ref)

    b_x = _dot(xn, w_in_ref[:, C_BX:C_BX + D_RNN])
    bc = cb_ref[...] + cw_ref[CONV_WIDTH - 1:CONV_WIDTH, :] * b_x
    for m in range(CONV_WIDTH - 1):
        bc = bc + cw_ref[m:m + 1, :] * rconv_st_ref[:, m * D_RNN:(m + 1) * D_RNN]
    rconv_o_ref[:, :(CONV_WIDTH - 2) * D_RNN] = rconv_st_ref[:, D_RNN:]
    rconv_o_ref[:, (CONV_WIDTH - 2) * D_RNN:] = b_x
    a_dec, b_inp = _lru_coeffs(bc, lru_w_ref, lru_b_ref, _lru_log_decay_scale(lam_ref[...]))
    h = a_dec * h0_ref[...] + b_inp
    h_o_ref[...] = h
    b_gate = _dot(xn, w_in_ref[:, C_BG:C_BG + D_RNN])
    pb = _dot(_bf(_gelu(b_gate) * h), w_pb_ref[...])

    u = _gelu(_dot(xn, w_in_ref[:, C_CU:C_CU + D_CHUNK]))
    v = _rms(_gelu(_dot(xn, w_in_ref[:, C_CV:C_CV + D_CHUNK])), vg_ref[...])
    v_o_ref[...] = v
    mix = ws0_ref[...] * v + bs0_ref[...]
    pc = _dot(_bf(u * mix), w_pc_ref[...])

    merged = _sigmoid(_dot(xn, w_in_ref[:, C_GA:C_GA + D_MODEL])) * pa
    merged = merged + _sigmoid(_dot(xn, w_in_ref[:, C_GB:C_GB + D_MODEL])) * pb
    merged = merged + _sigmoid(_dot(xn, w_in_ref[:, C_GC:C_GC + D_MODEL])) * pc
    xo_ref[...] = x + _dot(_bf(merged), w_o_ref[...])


def _sample_mixer(xs, pool_st, rconv_st, h0, w, l):
    n = xs.shape[0]
    f32 = jnp.float32
    acts = [xs, pool_st, rconv_st, h0]
    weights = [w[k] for k in _MIXER_PARAMS + ('ws0', 'bs0') + _PROJ_PARAMS]
    out_shape = [jax.ShapeDtypeStruct((n, D_MODEL), f32),
                 jax.ShapeDtypeStruct((n, POOL_STATE * D_POOL), f32),
                 jax.ShapeDtypeStruct((n, (CONV_WIDTH - 1) * D_RNN), f32),
                 jax.ShapeDtypeStruct((n, D_RNN), f32),
                 jax.ShapeDtypeStruct((n, D_CHUNK), f32)]
    return pl.pallas_call(
        _sample_mixer_kernel,
        grid=(1,),
        in_specs=[_whole_spec(xs)] + [_layer_spec(a, l) for a in acts[1:] + weights],
        out_specs=[pl.BlockSpec(s.shape, lambda t: (0, 0)) for s in out_shape],
        out_shape=out_shape,
        compiler_params=pltpu.CompilerParams(dimension_semantics=("arbitrary",),
                                             vmem_limit_bytes=VMEM_LIMIT),
        name="sample_mixer",
    )(*acts, *weights)


def _sample_ffn_kernel(x_ref, st_ref, g2_ref, wg_ref, wu_ref, fcw_ref, fcb_ref, wd_ref, gf_ref,
                       xo_ref, st_o_ref, *, last_layer):
    x = x_ref[...]
    xn = _bf(_rms(x, g2_ref[...]))
    g_pre = _dot(xn, wg_ref[...])
    y = fcb_ref[...] + fcw_ref[FFN_CONV - 1:FFN_CONV, :] * g_pre
    for m in range(FFN_CONV - 1):
        y = y + fcw_ref[m:m + 1, :] * st_ref[:, m * D_FF:(m + 1) * D_FF]
    st_o_ref[:, :(FFN_CONV - 2) * D_FF] = st_ref[:, D_FF:]
    st_o_ref[:, (FFN_CONV - 2) * D_FF:] = g_pre
    h = _gelu(y) * _dot(xn, wu_ref[...])
    out = x + _dot(_bf(h), wd_ref[...])
    if last_layer:
        out = _rms(out, gf_ref[...])
    xo_ref[...] = out


def _sample_ffn(xs, st, w, l, last_layer):
    n = xs.shape[0]
    weights = [w[k] for k in _FFN_PARAMS]
    out_shape = [jax.ShapeDtypeStruct((n, D_MODEL), jnp.float32),
                 jax.ShapeDtypeStruct((n, (FFN_CONV - 1) * D_FF), jnp.float32)]
    return pl.pallas_call(
        functools.partial(_sample_ffn_kernel, last_layer=last_layer),
        grid=(1,),
        in_specs=[_whole_spec(xs), _layer_spec(st, l)] + [_layer_spec(a, l) for a in weights]
                 + [_whole_spec(w['gf'])],
        out_specs=[pl.BlockSpec(s.shape, lambda t: (0, 0)) for s in out_shape],
        out_shape=out_shape,
        compiler_params=pltpu.CompilerParams(dimension_semantics=("arbitrary",),
                                             vmem_limit_bytes=VMEM_LIMIT),
        name="sample_ffn",
    )(xs, st, *weights, w['gf'])


def _prepare_weights(p):
    bf16 = jnp.bfloat16
    depth = p['w_in'].shape[0]
    row = lambda a: a.reshape(depth, 1, -1)
    return {
        'g1': row(p['norm1_g']), 'w_in': p['w_in'].astype(bf16),
        'pool_w': p['pool_w'].astype(bf16), 'pool_scale': row(p['pool_scale']),
        'cw': p['rnn_conv_w'], 'cb': row(p['rnn_conv_b']),
        'lru_w': jnp.concatenate([p['lru_wa'], p['lru_wx']], axis=-1).astype(bf16),
        'lru_b': jnp.stack([p['lru_ba'], p['lru_bx']], axis=1),
        'lam': row(p['lru_lambda']), 'vg': row(p['chunk_vnorm_g']),
        'ws': p['chunk_ws'].astype(bf16),
        'bs_tm': jnp.repeat(jnp.swapaxes(p['chunk_bs'], 1, 2), PB, axis=1),
        'ws0': row(jnp.repeat(p['chunk_ws'][:, :, 0, 0], CHUNK_GW, axis=1)),
        'bs0': row(jnp.repeat(p['chunk_bs'][:, :, 0], CHUNK_GW, axis=1)),
        'w_pa': p['w_pa'].astype(bf16), 'w_pb': p['w_pb'].astype(bf16),
        'w_pc': p['w_pc'].astype(bf16), 'w_o': p['w_o'].astype(bf16),
        'g2': row(p['norm2_g']), 'wg': p['ffn_wg'].astype(bf16),
        'wu': p['ffn_wu'].astype(bf16), 'fcw': p['ffn_conv_w'],
        'fcb': row(p['ffn_conv_b']), 'wd': p['ffn_wd'].astype(bf16),
        'gf': p['final_norm_g'].reshape(1, -1),
    }


def _batch_major(rows_tm, n):
    return jnp.transpose(rows_tm.reshape(-1, PB, rows_tm.shape[-1])[-n:], (1, 0, 2))


def kernel(x_prompt, x_sample, state_pool, state_rnn_conv, state_rnn_h, state_ffn_conv, norm1_g, w_in, pool_w, pool_scale, rnn_conv_w, rnn_conv_b, lru_wa, lru_ba, lru_wx, lru_bx, lru_lambda, chunk_vnorm_g, chunk_ws, chunk_bs, w_pa, w_pb, w_pc, w_o, norm2_g, ffn_wg, ffn_wu, ffn_conv_w, ffn_conv_b, ffn_wd, final_norm_g):
    p = dict(norm1_g=norm1_g, w_in=w_in, pool_w=pool_w, pool_scale=pool_scale,
             rnn_conv_w=rnn_conv_w, rnn_conv_b=rnn_conv_b, lru_wa=lru_wa, lru_ba=lru_ba,
             lru_wx=lru_wx, lru_bx=lru_bx, lru_lambda=lru_lambda, chunk_vnorm_g=chunk_vnorm_g,
             chunk_ws=chunk_ws, chunk_bs=chunk_bs, w_pa=w_pa, w_pb=w_pb, w_pc=w_pc, w_o=w_o,
             norm2_g=norm2_g, ffn_wg=ffn_wg, ffn_wu=ffn_wu, ffn_conv_w=ffn_conv_w,
             ffn_conv_b=ffn_conv_b, ffn_wd=ffn_wd, final_norm_g=final_norm_g)
    depth = w_in.shape[0]
    bp, tp, _ = x_prompt.shape
    ns = x_sample.shape[0]
    assert bp == PB and tp % TILE_T == 0 and x_sample.shape[1] == 1
    w = _prepare_weights(p)

    xp = x_prompt
    xs = x_sample.reshape(ns, D_MODEL)
    pool_st = state_pool.reshape(depth, ns, -1)
    rconv_st = state_rnn_conv.reshape(depth, ns, -1)
    ffn_st = state_ffn_conv.reshape(depth, ns, -1)
    outs = {k: [] for k in ('pool_p', 'pool_s', 'rc_p', 'rc_s', 'h_p', 'h_s', 'ff_p', 'ff_s', 'cv_s')}
    for l in range(depth):
        last = l == depth - 1
        xp, pool_tm, rconv_tm, h_p = _prompt_mixer(xp, w, l, batch_major_in=(l == 0))
        xp, ffn_tm = _prompt_ffn(xp, w, l, last)
        xs, pool_s, rconv_s, h_s, v_s = _sample_mixer(xs, pool_st, rconv_st, state_rnn_h, w, l)
        xs, ffn_s = _sample_ffn(xs, ffn_st, w, l, last)
        outs['pool_p'].append(_batch_major(pool_tm, POOL_STATE))
        outs['rc_p'].append(_batch_major(rconv_tm, CONV_WIDTH - 1))
        outs['h_p'].append(h_p)
        outs['ff_p'].append(_batch_major(ffn_tm, FFN_CONV - 1))
        outs['pool_s'].append(pool_s.reshape(ns, POOL_STATE, D_POOL))
        outs['rc_s'].append(rconv_s.reshape(ns, CONV_WIDTH - 1, D_RNN))
        outs['h_s'].append(h_s)
        outs['ff_s'].append(ffn_s.reshape(ns, FFN_CONV - 1, D_FF))
        outs['cv_s'].append(v_s.reshape(ns, 1, D_CHUNK))
    st = jnp.stack
    return (xp, xs.reshape(ns, 1, D_MODEL), st(outs['pool_p']), st(outs['pool_s']),
            st(outs['rc_p']), st(outs['rc_s']), st(outs['h_p']), st(outs['h_s']),
            st(outs['ff_p']), st(outs['ff_s']), st(outs['cv_s']))
```

```python
import functools
import math

import jax
import jax.numpy as jnp
from jax import lax
from jax.experimental import pallas as pl
from jax.experimental.pallas import tpu as pltpu

D_MODEL = 1024
D_POOL = 512
POOL_WINDOWS = (2, 4, 8, 16)
POOL_GW = 128
POOL_STATE = 15
D_RNN = 1024
RNN_HEADS = 8
RNN_HD = 128
CONV_WIDTH = 4
LRU_C = 8.0
D_CHUNK = 512
CHUNK = 128
CHUNK_GROUPS = 4
CHUNK_GW = 128
D_FF = 3072
FFN_CONV = 3
EPS = 1e-6
PAST_LEN = 16384

C_POOL = 0
C_BX = D_POOL
C_BG = C_BX + D_RNN
C_CU = C_BG + D_RNN
C_CV = C_CU + D_CHUNK
C_GA = C_CV + D_CHUNK
C_GB = C_GA + D_MODEL
C_GC = C_GB + D_MODEL

PB = 8
PB_SHIFT = PB.bit_length() - 1
TILE_T = CHUNK
TILE_R = TILE_T * PB
SUB_T = 64
SUB_R = SUB_T * PB
FFN_R = 512
FFN_T = FFN_R // PB
FFN_C = 1536
VMEM_LIMIT = 60 * 1024 * 1024

_GELU_K = math.sqrt(2.0 / math.pi)


def _dot(a, b):
    return jnp.dot(a, b, preferred_element_type=jnp.float32)


def _bf(x):
    return x.astype(jnp.bfloat16)


def _gelu(x):
    return 0.5 * x * (1.0 + jnp.tanh(_GELU_K * (x + 0.044715 * (x * x * x))))


def _sigmoid(x):
    return 0.5 * (jnp.tanh(0.5 * x) + 1.0)


def _rms(x, g):
    return x * lax.rsqrt(jnp.mean(x * x, axis=-1, keepdims=True) + EPS) * g


def _lru_log_decay_scale(lam):
    z = -lam
    sp = jnp.maximum(z, 0.0) + jnp.log1p(jnp.exp(-jnp.abs(z)))
    return -LRU_C * sp


def _lru_coeffs(bc, lru_w_ref, lru_b_ref, dscale):
    a_parts, b_parts = [], []
    for h in range(RNN_HEADS):
        cs = slice(h * RNN_HD, (h + 1) * RNN_HD)
        bch = bc[:, cs]
        ri = _dot(_bf(bch), lru_w_ref[h])
        r = _sigmoid(ri[:, :RNN_HD] + lru_b_ref[0:1, cs])
        i = _sigmoid(ri[:, RNN_HD:] + lru_b_ref[1:2, cs])
        log_a = dscale[:, cs] * r
        a = jnp.exp(log_a)
        a_parts.append(a)
        b_parts.append(jnp.sqrt((1.0 - a) * (1.0 + a)) * (i * bch))
    return jnp.concatenate(a_parts, axis=1), jnp.concatenate(b_parts, axis=1)


def _pool_project(d_parts, pool_w_ref, pool_scale_ref, w_pa_ref):
    ya = [_dot(_bf(d), pool_w_ref[g]) * pool_scale_ref[:, g * POOL_GW:(g + 1) * POOL_GW]
          for g, d in enumerate(d_parts)]
    return _dot(_bf(jnp.concatenate(ya, axis=1)), w_pa_ref[...])


def _build_time_major_spatial_weights(ws_ref, kron_ref):
    row_t = lax.shift_right_logical(lax.broadcasted_iota(jnp.int32, (SUB_R, SUB_T), 0), PB_SHIFT)
    repeat_rows = _bf(row_t == lax.broadcasted_iota(jnp.int32, (SUB_R, SUB_T), 1))
    col_t = lax.shift_right_logical(lax.broadcasted_iota(jnp.int32, (CHUNK, TILE_R), 1), PB_SHIFT)
    repeat_cols = _bf(col_t == lax.broadcasted_iota(jnp.int32, (CHUNK, TILE_R), 0))
    rr = lax.broadcasted_iota(jnp.int32, (SUB_R, TILE_R), 0)
    cc = lax.broadcasted_iota(jnp.int32, (SUB_R, TILE_R), 1)
    same_seq = (rr & (PB - 1)) == (cc & (PB - 1))
    for k in range(TILE_T // SUB_T):
        causal = (lax.shift_right_logical(rr, PB_SHIFT) + k * SUB_T) >= lax.shift_right_logical(cc, PB_SHIFT)
        keep = same_seq & causal
        for g in range(CHUNK_GROUPS):
            rows = _dot(repeat_rows, ws_ref[g, k * SUB_T:(k + 1) * SUB_T, :])
            full = _dot(_bf(rows), repeat_cols)
            kron_ref[g, k * SUB_R:(k + 1) * SUB_R, :] = _bf(jnp.where(keep, full, 0.0))


def _prompt_mixer_kernel(x_ref, g1_ref, w_in_ref, pool_w_ref, pool_scale_ref, cw_ref, cb_ref,
                         lru_w_ref, lru_b_ref, lam_ref, vg_ref, ws_ref, bs_ref,
                         w_pa_ref, w_pb_ref, w_pc_ref, w_o_ref,
                         xo_ref, pool_o_ref, rconv_o_ref, h_o_ref,
                         pool_prev, conv_prev, h_carry, kron_ref,
                         *, batch_major_in):
    t = pl.program_id(0)

    @pl.when(t == 0)
    def _():
        pool_prev[...] = jnp.zeros_like(pool_prev)
        conv_prev[...] = jnp.zeros_like(conv_prev)
        h_carry[...] = jnp.zeros_like(h_carry)
        _build_time_major_spatial_weights(ws_ref, kron_ref)

    dscale = _lru_log_decay_scale(lam_ref[...])
    g1 = g1_ref[...]
    pool_pre = pool_prev[...]
    conv_pre = conv_prev[...]
    h = h_carry[...]
    npre = pool_pre.shape[0]
    v_parts = []

    for k in range(TILE_T // SUB_T):
        r0 = k * SUB_R
        if batch_major_in:
            x = pltpu.einshape("sjc->jsc", x_ref[:, k * SUB_T:(k + 1) * SUB_T, :]).reshape(SUB_R, D_MODEL)
        else:
            x = x_ref[r0:r0 + SUB_R, :]
        xn = _bf(_rms(x, g1))
        proj = lambda c0, n: _dot(xn, w_in_ref[:, c0:c0 + n])

        a_in = proj(C_POOL, D_POOL)
        a_ext = jnp.concatenate([pool_pre, a_in], axis=0)
        pos = t * TILE_T + k * SUB_T + lax.shift_right_logical(
            lax.broadcasted_iota(jnp.int32, (SUB_R, POOL_GW), 0), PB_SHIFT)
        d_parts = []
        for g, w in enumerate(POOL_WINDOWS):
            cs = slice(g * POOL_GW, (g + 1) * POOL_GW)
            s = a_ext[npre - (w - 1) * PB:, cs]
            sh = 1
            while sh < w:
                s = s[sh * PB:] + s[:-sh * PB]
                sh *= 2
            cnt = jnp.minimum(pos + 1, w).astype(jnp.float32)
            d_parts.append(s / cnt - a_in[:, cs])
        pool_pre = a_ext[SUB_R:]
        pa = _pool_project(d_parts, pool_w_ref, pool_scale_ref, w_pa_ref)

        b_x = proj(C_BX, D_RNN)
        b_ext = jnp.concatenate([conv_pre, b_x], axis=0)
        bc = cb_ref[...] + cw_ref[CONV_WIDTH - 1:CONV_WIDTH, :] * b_x
        for m in range(1, CONV_WIDTH):
            lo = (CONV_WIDTH - 1 - m) * PB
            bc = bc + cw_ref[CONV_WIDTH - 1 - m:CONV_WIDTH - m, :] * b_ext[lo:lo + SUB_R]
        conv_pre = b_ext[SUB_R:]
        a_dec, b_inp = _lru_coeffs(bc, lru_w_ref, lru_b_ref, dscale)
        hs = []
        for j in range(SUB_T):
            h = a_dec[j * PB:(j + 1) * PB] * h + b_inp[j * PB:(j + 1) * PB]
            hs.append(h)
        b_gate = proj(C_BG, D_RNN)
        pb = _dot(_bf(_gelu(b_gate) * jnp.concatenate(hs, axis=0)), w_pb_ref[...])

        u = _gelu(proj(C_CU, D_CHUNK))
        v = _rms(_gelu(proj(C_CV, D_CHUNK)), vg_ref[...])
        v_parts.append(_bf(v))
        v_seen = jnp.concatenate(v_parts, axis=0)
        mix = []
        for g in range(CHUNK_GROUPS):
            cs = slice(g * CHUNK_GW, (g + 1) * CHUNK_GW)
            m = _dot(kron_ref[g, r0:r0 + SUB_R, 0:r0 + SUB_R], v_seen[:, cs])
            mix.append(m + bs_ref[r0:r0 + SUB_R, g:g + 1])
        pc = _dot(_bf(u * jnp.concatenate(mix, axis=1)), w_pc_ref[...])

        merged = _sigmoid(proj(C_GA, D_MODEL)) * pa
        merged = merged + _sigmoid(proj(C_GB, D_MODEL)) * pb
        merged = merged + _sigmoid(proj(C_GC, D_MODEL)) * pc
        xo_ref[r0:r0 + SUB_R, :] = x + _dot(_bf(merged), w_o_ref[...])

    pool_prev[...] = pool_pre
    conv_prev[...] = conv_pre
    h_carry[...] = h
    pool_o_ref[...] = pool_pre
    rconv_o_ref[...] = conv_pre
    h_o_ref[...] = h


def _layer_spec(a, l):
    nd = a.ndim
    return pl.BlockSpec((None,) + a.shape[1:], lambda *_: (l,) + (0,) * (nd - 1),
                        pipeline_mode=pl.Buffered(1))


def _whole_spec(a):
    nd = a.ndim
    return pl.BlockSpec(a.shape, lambda *_: (0,) * nd, pipeline_mode=pl.Buffered(1))


_MIXER_PARAMS = ('g1', 'w_in', 'pool_w', 'pool_scale', 'cw', 'cb', 'lru_w', 'lru_b', 'lam', 'vg')
_PROJ_PARAMS = ('w_pa', 'w_pb', 'w_pc', 'w_o')
_FFN_PARAMS = ('g2', 'wg', 'wu', 'fcw', 'fcb', 'wd')


def _prompt_mixer(x, w, l, batch_major_in):
    if batch_major_in:
        n_rows = x.shape[0] * x.shape[1]
        x_spec = pl.BlockSpec((PB, TILE_T, D_MODEL), lambda t: (0, t, 0))
    else:
        n_rows = x.shape[0]
        x_spec = pl.BlockSpec((TILE_R, D_MODEL), lambda t: (t, 0))
    weights = [w[k] for k in _MIXER_PARAMS + ('ws', 'bs_tm') + _PROJ_PARAMS]
    pool_rows = (POOL_STATE + 1) * PB
    conv_rows = (CONV_WIDTH - 1) * PB
    f32 = jnp.float32
    return pl.pallas_call(
        functools.partial(_prompt_mixer_kernel, batch_major_in=batch_major_in),
        grid=(n_rows // TILE_R,),
        in_specs=[x_spec] + [_layer_spec(a, l) for a in weights],
        out_specs=[pl.BlockSpec((TILE_R, D_MODEL), lambda t: (t, 0)),
                   pl.BlockSpec((pool_rows, D_POOL), lambda t: (0, 0)),
                   pl.BlockSpec((conv_rows, D_RNN), lambda t: (0, 0)),
                   pl.BlockSpec((PB, D_RNN), lambda t: (0, 0))],
        out_shape=[jax.ShapeDtypeStruct((n_rows, D_MODEL), f32),
                   jax.ShapeDtypeStruct((pool_rows, D_POOL), f32),
                   jax.ShapeDtypeStruct((conv_rows, D_RNN), f32),
                   jax.ShapeDtypeStruct((PB, D_RNN), f32)],
        scratch_shapes=[pltpu.VMEM((pool_rows, D_POOL), f32),
                        pltpu.VMEM((conv_rows, D_RNN), f32),
                        pltpu.VMEM((PB, D_RNN), f32),
                        pltpu.VMEM((CHUNK_GROUPS, TILE_R, TILE_R), jnp.bfloat16)],
        compiler_params=pltpu.CompilerParams(dimension_semantics=("arbitrary",),
                                             vmem_limit_bytes=VMEM_LIMIT),
        name="prompt_mixer",
    )(x, *weights)


def _prompt_ffn_kernel(x_ref, g2_ref, wg_ref, wu_ref, fcw_ref, fcb_ref, wd_ref, gf_ref,
                       xo_ref, ffn_o_ref, g_prev, *, last_layer):
    t = pl.program_id(0)

    @pl.when(t == 0)
    def _():
        g_prev[...] = jnp.zeros_like(g_prev)

    x = x_ref[...]
    xn = _bf(_rms(x, g2_ref[...]))
    acc = x
    npre = (FFN_CONV - 1) * PB
    n_chunks = D_FF // FFN_C
    col = lambda c: slice(c * FFN_C, (c + 1) * FFN_C)
    up_proj = lambda c: (_dot(xn, wg_ref[:, col(c)]), _dot(xn, wu_ref[:, col(c)]))
    nxt = up_proj(0)
    for c in range(n_chunks):
        cs = col(c)
        g_pre, up = nxt
        if c + 1 < n_chunks:
            nxt = up_proj(c + 1)
        g_ext = jnp.concatenate([g_prev[:, cs], g_pre], axis=0)
        y = fcb_ref[:, cs] + fcw_ref[FFN_CONV - 1:FFN_CONV, cs] * g_pre
        for m in range(1, FFN_CONV):
            lo = npre - m * PB
            y = y + fcw_ref[FFN_CONV - 1 - m:FFN_CONV - m, cs] * g_ext[lo:lo + FFN_R]
        g_prev[:, cs] = g_ext[FFN_R:]
        h = _gelu(y) * up
        acc = acc + _dot(_bf(h), wd_ref[cs, :])
    if last_layer:
        y = _rms(acc, gf_ref[...]).reshape(FFN_T, PB, D_MODEL)
        xo_ref[...] = pltpu.einshape("jsc->sjc", y)
    else:
        xo_ref[...] = acc
    ffn_o_ref[...] = g_prev[...]


def _prompt_ffn(x_tm, w, l, last_layer):
    n_rows = x_tm.shape[0]
    row_spec = pl.BlockSpec((FFN_R, D_MODEL), lambda t: (t, 0))
    pre_rows = (FFN_CONV - 1) * PB
    if last_layer:
        out_spec = pl.BlockSpec((PB, FFN_T, D_MODEL), lambda t: (0, t, 0))
        out_shape = jax.ShapeDtypeStruct((PB, n_rows // PB, D_MODEL), jnp.float32)
    else:
        out_spec, out_shape = row_spec, jax.ShapeDtypeStruct((n_rows, D_MODEL), jnp.float32)
    weights = [w[k] for k in _FFN_PARAMS]
    return pl.pallas_call(
        functools.partial(_prompt_ffn_kernel, last_layer=last_layer),
        grid=(n_rows // FFN_R,),
        in_specs=[row_spec] + [_layer_spec(a, l) for a in weights] + [_whole_spec(w['gf'])],
        out_specs=[out_spec, pl.BlockSpec((pre_rows, D_FF), lambda t: (0, 0))],
        out_shape=[out_shape, jax.ShapeDtypeStruct((pre_rows, D_FF), jnp.float32)],
        scratch_shapes=[pltpu.VMEM((pre_rows, D_FF), jnp.float32)],
        compiler_params=pltpu.CompilerParams(dimension_semantics=("arbitrary",),
                                             vmem_limit_bytes=VMEM_LIMIT),
        name="prompt_ffn",
    )(x_tm, *weights, w['gf'])


def _sample_mixer_kernel(x_ref, pool_st_ref, rconv_st_ref, h0_ref,
                         g1_ref, w_in_ref, pool_w_ref, pool_scale_ref, cw_ref, cb_ref,
                         lru_w_ref, lru_b_ref, lam_ref, vg_ref, ws0_ref, bs0_ref,
                         w_pa_ref, w_pb_ref, w_pc_ref, w_o_ref,
                         xo_ref, pool_o_ref, rconv_o_ref, h_o_ref, v_o_ref):
    x = x_ref[...]
    xn = _bf(_rms(x, g1_ref[...]))

    a_in = _dot(xn, w_in_ref[:, C_POOL:C_POOL + D_POOL])
    d_parts = []
    for g, w in enumerate(POOL_WINDOWS):
        s = a_in[:, g * POOL_GW:(g + 1) * POOL_GW]
        for m in range(1, w):
            s = s + pool_st_ref[POOL_STATE - m, :, g * POOL_GW:(g + 1) * POOL_GW]
        cnt = float(min(PAST_LEN + 1, w))
        d_parts.append(s / cnt - a_in[:, g * POOL_GW:(g + 1) * POOL_GW])
    pool_o_ref[:POOL_STATE - 1] = pool_st_ref[1:]
    pool_o_ref[POOL_STATE - 1] = a_in
    pa = _pool_project(d_parts, pool_w_ref, pool_scale_ref, w_pa_ref)

    b_x = _dot(xn, w_in_ref[:, C_BX:C_BX + D_RNN])
    bc = cb_ref[...] + cw_ref[CONV_WIDTH - 1:CONV_WIDTH, :] * b_x
    for m in range(CONV_WIDTH - 1):
        bc = bc + cw_ref[m:m + 1, :] * rconv_st_ref[m]
    rconv_o_ref[:CONV_WIDTH - 2] = rconv_st_ref[1:]
    rconv_o_ref[CONV_WIDTH - 2] = b_x
    a_dec, b_inp = _lru_coeffs(bc, lru_w_ref, lru_b_ref, _lru_log_decay_scale(lam_ref[...]))
    h = a_dec * h0_ref[...] + b_inp
    h_o_ref[...] = h
    b_gate = _dot(xn, w_in_ref[:, C_BG:C_BG + D_RNN])
    pb = _dot(_bf(_gelu(b_gate) * h), w_pb_ref[...])

    u = _gelu(_dot(xn, w_in_ref[:, C_CU:C_CU + D_CHUNK]))
    v = _rms(_gelu(_dot(xn, w_in_ref[:, C_CV:C_CV + D_CHUNK])), vg_ref[...])
    v_o_ref[...] = v
    mix = ws0_ref[...] * v + bs0_ref[...]
    pc = _dot(_bf(u * mix), w_pc_ref[...])

    merged = _sigmoid(_dot(xn, w_in_ref[:, C_GA:C_GA + D_MODEL])) * pa
    merged = merged + _sigmoid(_dot(xn, w_in_ref[:, C_GB:C_GB + D_MODEL])) * pb
    merged = merged + _sigmoid(_dot(xn, w_in_ref[:, C_GC:C_GC + D_MODEL])) * pc
    xo_ref[...] = x + _dot(_bf(merged), w_o_ref[...])


def _sample_mixer(xs, pool_st, rconv_st, h0, w, l):
    n = xs.shape[0]
    f32 = jnp.float32
    acts = [xs, pool_st, rconv_st, h0]
    weights = [w[k] for k in _MIXER_PARAMS + ('ws0', 'bs0') + _PROJ_PARAMS]
    out_shape = [jax.ShapeDtypeStruct((n, D_MODEL), f32),
                 jax.ShapeDtypeStruct((POOL_STATE, n, D_POOL), f32),
                 jax.ShapeDtypeStruct((CONV_WIDTH - 1, n, D_RNN), f32),
                 jax.ShapeDtypeStruct((n, D_RNN), f32),
                 jax.ShapeDtypeStruct((n, D_CHUNK), f32)]
    return pl.pallas_call(
        _sample_mixer_kernel,
        grid=(1,),
        in_specs=[_whole_spec(xs)] + [_layer_spec(a, l) for a in acts[1:] + weights],
        out_specs=[pl.BlockSpec(s.shape, lambda t, nd=len(s.shape): (0,) * nd) for s in out_shape],
        out_shape=out_shape,
        compiler_params=pltpu.CompilerParams(dimension_semantics=("arbitrary",),
                                             vmem_limit_bytes=VMEM_LIMIT),
        name="sample_mixer",
    )(*acts, *weights)


def _sample_ffn_kernel(x_ref, st_ref, g2_ref, wg_ref, wu_ref, fcw_ref, fcb_ref, wd_ref, gf_ref,
                       xo_ref, st_o_ref, *, last_layer):
    x = x_ref[...]
    xn = _bf(_rms(x, g2_ref[...]))
    g_pre = _dot(xn, wg_ref[...])
    y = fcb_ref[...] + fcw_ref[FFN_CONV - 1:FFN_CONV, :] * g_pre
    for m in range(FFN_CONV - 1):
        y = y + fcw_ref[m:m + 1, :] * st_ref[m]
    st_o_ref[:FFN_CONV - 2] = st_ref[1:]
    st_o_ref[FFN_CONV - 2] = g_pre
    h = _gelu(y) * _dot(xn, wu_ref[...])
    out = x + _dot(_bf(h), wd_ref[...])
    if last_layer:
        out = _rms(out, gf_ref[...])
    xo_ref[...] = out


def _sample_ffn(xs, st, w, l, last_layer):
    n = xs.shape[0]
    weights = [w[k] for k in _FFN_PARAMS]
    out_shape = [jax.ShapeDtypeStruct((n, D_MODEL), jnp.float32),
                 jax.ShapeDtypeStruct((FFN_CONV - 1, n, D_FF), jnp.float32)]
    return pl.pallas_call(
        functools.partial(_sample_ffn_kernel, last_layer=last_layer),
        grid=(1,),
        in_specs=[_whole_spec(xs), _layer_spec(st, l)] + [_layer_spec(a, l) for a in weights]
                 + [_whole_spec(w['gf'])],
        out_specs=[pl.BlockSpec(s.shape, lambda t, nd=len(s.shape): (0,) * nd) for s in out_shape],
        out_shape=out_shape,
        compiler_params=pltpu.CompilerParams(dimension_semantics=("arbitrary",),
                                             vmem_limit_bytes=VMEM_LIMIT),
        name="sample_ffn",
    )(xs, st, *weights, w['gf'])


def _prepare_weights(p):
    bf16 = jnp.bfloat16
    depth = p['w_in'].shape[0]
    row = lambda a: a.reshape(depth, 1, -1)
    return {
        'g1': row(p['norm1_g']), 'w_in': p['w_in'].astype(bf16),
        'pool_w': p['pool_w'].astype(bf16), 'pool_scale': row(p['pool_scale']),
        'cw': p['rnn_conv_w'], 'cb': row(p['rnn_conv_b']),
        'lru_w': jnp.concatenate([p['lru_wa'], p['lru_wx']], axis=-1).astype(bf16),
        'lru_b': jnp.stack([p['lru_ba'], p['lru_bx']], axis=1),
        'lam': row(p['lru_lambda']), 'vg': row(p['chunk_vnorm_g']),
        'ws': p['chunk_ws'].astype(bf16),
        'bs_tm': jnp.repeat(jnp.swapaxes(p['chunk_bs'], 1, 2), PB, axis=1),
        'ws0': row(jnp.repeat(p['chunk_ws'][:, :, 0, 0], CHUNK_GW, axis=1)),
        'bs0': row(jnp.repeat(p['chunk_bs'][:, :, 0], CHUNK_GW, axis=1)),
        'w_pa': p['w_pa'].astype(bf16), 'w_pb': p['w_pb'].astype(bf16),
        'w_pc': p['w_pc'].astype(bf16), 'w_o': p['w_o'].astype(bf16),
        'g2': row(p['norm2_g']), 'wg': p['ffn_wg'].astype(bf16),
        'wu': p['ffn_wu'].astype(bf16), 'fcw': p['ffn_conv_w'],
        'fcb': row(p['ffn_conv_b']), 'wd': p['ffn_wd'].astype(bf16),
        'gf': p['final_norm_g'].reshape(1, -1),
    }


def _batch_major(rows_tm, n):
    return jnp.transpose(rows_tm.reshape(-1, PB, rows_tm.shape[-1])[-n:], (1, 0, 2))


def kernel(x_prompt, x_sample, state_pool, state_rnn_conv, state_rnn_h, state_ffn_conv, norm1_g, w_in, pool_w, pool_scale, rnn_conv_w, rnn_conv_b, lru_wa, lru_ba, lru_wx, lru_bx, lru_lambda, chunk_vnorm_g, chunk_ws, chunk_bs, w_pa, w_pb, w_pc, w_o, norm2_g, ffn_wg, ffn_wu, ffn_conv_w, ffn_conv_b, ffn_wd, final_norm_g):
    p = dict(norm1_g=norm1_g, w_in=w_in, pool_w=pool_w, pool_scale=pool_scale,
             rnn_conv_w=rnn_conv_w, rnn_conv_b=rnn_conv_b, lru_wa=lru_wa, lru_ba=lru_ba,
             lru_wx=lru_wx, lru_bx=lru_bx, lru_lambda=lru_lambda, chunk_vnorm_g=chunk_vnorm_g,
             chunk_ws=chunk_ws, chunk_bs=chunk_bs, w_pa=w_pa, w_pb=w_pb, w_pc=w_pc, w_o=w_o,
             norm2_g=norm2_g, ffn_wg=ffn_wg, ffn_wu=ffn_wu, ffn_conv_w=ffn_conv_w,
             ffn_conv_b=ffn_conv_b, ffn_wd=ffn_wd, final_norm_g=final_norm_g)
    depth = w_in.shape[0]
    bp, tp, _ = x_prompt.shape
    ns = x_sample.shape[0]
    assert bp == PB and tp % TILE_T == 0 and x_sample.shape[1] == 1
    w = _prepare_weights(p)

    xp = x_prompt
    xs = x_sample.reshape(ns, D_MODEL)
    pool_st = jnp.swapaxes(state_pool, 1, 2)
    rconv_st = jnp.swapaxes(state_rnn_conv, 1, 2)
    ffn_st = jnp.swapaxes(state_ffn_conv, 1, 2)
    outs = {k: [] for k in ('pool_p', 'pool_s', 'rc_p', 'rc_s', 'h_p', 'h_s', 'ff_p', 'ff_s', 'cv_s')}
    for l in range(depth):
        last = l == depth - 1
        xp, pool_tm, rconv_tm, h_p = _prompt_mixer(xp, w, l, batch_major_in=(l == 0))
        xp, ffn_tm = _prompt_ffn(xp, w, l, last)
        xs, pool_s, rconv_s, h_s, v_s = _sample_mixer(xs, pool_st, rconv_st, state_rnn_h, w, l)
        xs, ffn_s = _sample_ffn(xs, ffn_st, w, l, last)
        outs['pool_p'].append(_batch_major(pool_tm, POOL_STATE))
        outs['rc_p'].append(_batch_major(rconv_tm, CONV_WIDTH - 1))
        outs['h_p'].append(h_p)
        outs['ff_p'].append(_batch_major(ffn_tm, FFN_CONV - 1))
        outs['pool_s'].append(jnp.swapaxes(pool_s, 0, 1))
        outs['rc_s'].append(jnp.swapaxes(rconv_s, 0, 1))
        outs['h_s'].append(h_s)
        outs['ff_s'].append(jnp.swapaxes(ffn_s, 0, 1))
        outs['cv_s'].append(v_s.reshape(ns, 1, D_CHUNK))
    st = jnp.stack
    return (xp, xs.reshape(ns, 1, D_MODEL), st(outs['pool_p']), st(outs['pool_s']),
            st(outs['rc_p']), st(outs['rc_s']), st(outs['h_p']), st(outs['h_s']),
            st(outs['ff_p']), st(outs['ff_s']), st(outs['cv_s']))
```

```python
import functools
import math

import jax
import jax.numpy as jnp
from jax import lax
from jax.experimental import pallas as pl
from jax.experimental.pallas import tpu as pltpu

D_MODEL = 1024
D_POOL = 512
POOL_WINDOWS = (2, 4, 8, 16)
POOL_GW = 128
POOL_STATE = 15
D_RNN = 1024
RNN_HEADS = 8
RNN_HD = 128
CONV_WIDTH = 4
LRU_C = 8.0
D_CHUNK = 512
CHUNK = 128
CHUNK_GROUPS = 4
CHUNK_GW = 128
D_FF = 3072
FFN_CONV = 3
EPS = 1e-6
PAST_LEN = 16384

C_POOL = 0
C_BX = D_POOL
C_BG = C_BX + D_RNN
C_CU = C_BG + D_RNN
C_CV = C_CU + D_CHUNK
C_GA = C_CV + D_CHUNK
C_GB = C_GA + D_MODEL
C_GC = C_GB + D_MODEL

PB = 8
PB_SHIFT = PB.bit_length() - 1
TILE_T = CHUNK
TILE_R = TILE_T * PB
SUB_T = 64
SUB_R = SUB_T * PB
FFN_R = 512
FFN_T = FFN_R // PB
FFN_C = 1536
VMEM_LIMIT = 60 * 1024 * 1024

_GELU_K = math.sqrt(2.0 / math.pi)


def _dot(a, b):
    return jnp.dot(a, b, preferred_element_type=jnp.float32)


def _bf(x):
    return x.astype(jnp.bfloat16)


def _gelu(x):
    return 0.5 * x * (1.0 + jnp.tanh(_GELU_K * (x + 0.044715 * (x * x * x))))


def _sigmoid(x):
    return 0.5 * (jnp.tanh(0.5 * x) + 1.0)


def _rms(x, g):
    return x * lax.rsqrt(jnp.mean(x * x, axis=-1, keepdims=True) + EPS) * g


def _lru_log_decay_scale(lam):
    z = -lam
    sp = jnp.maximum(z, 0.0) + jnp.log1p(jnp.exp(-jnp.abs(z)))
    return -LRU_C * sp


def _lru_coeffs(bc, lru_w_ref, lru_b_ref, dscale, after_head=None):
    a_parts, b_parts = [], []
    for h in range(RNN_HEADS):
        if after_head is not None and h > 0:
            after_head(h - 1)
        cs = slice(h * RNN_HD, (h + 1) * RNN_HD)
        bch = bc[:, cs]
        ri = _dot(_bf(bch), lru_w_ref[h])
        r = _sigmoid(ri[:, :RNN_HD] + lru_b_ref[0:1, cs])
        i = _sigmoid(ri[:, RNN_HD:] + lru_b_ref[1:2, cs])
        log_a = dscale[:, cs] * r
        a = jnp.exp(log_a)
        a_parts.append(a)
        b_parts.append(jnp.sqrt((1.0 - a) * (1.0 + a)) * (i * bch))
    return jnp.concatenate(a_parts, axis=1), jnp.concatenate(b_parts, axis=1)


def _pool_project(d_parts, pool_w_ref, pool_scale_ref, w_pa_ref):
    ya = [_dot(_bf(d), pool_w_ref[g]) * pool_scale_ref[:, g * POOL_GW:(g + 1) * POOL_GW]
          for g, d in enumerate(d_parts)]
    return _dot(_bf(jnp.concatenate(ya, axis=1)), w_pa_ref[...])


def _build_time_major_spatial_weights(ws_ref, kron_ref):
    row_t = lax.shift_right_logical(lax.broadcasted_iota(jnp.int32, (SUB_R, SUB_T), 0), PB_SHIFT)
    repeat_rows = _bf(row_t == lax.broadcasted_iota(jnp.int32, (SUB_R, SUB_T), 1))
    col_t = lax.shift_right_logical(lax.broadcasted_iota(jnp.int32, (CHUNK, TILE_R), 1), PB_SHIFT)
    repeat_cols = _bf(col_t == lax.broadcasted_iota(jnp.int32, (CHUNK, TILE_R), 0))
    rr = lax.broadcasted_iota(jnp.int32, (SUB_R, TILE_R), 0)
    cc = lax.broadcasted_iota(jnp.int32, (SUB_R, TILE_R), 1)
    same_seq = (rr & (PB - 1)) == (cc & (PB - 1))
    for k in range(TILE_T // SUB_T):
        causal = (lax.shift_right_logical(rr, PB_SHIFT) + k * SUB_T) >= lax.shift_right_logical(cc, PB_SHIFT)
        keep = same_seq & causal
        for g in range(CHUNK_GROUPS):
            rows = _dot(repeat_rows, ws_ref[g, k * SUB_T:(k + 1) * SUB_T, :])
            full = _dot(_bf(rows), repeat_cols)
            kron_ref[g, k * SUB_R:(k + 1) * SUB_R, :] = _bf(jnp.where(keep, full, 0.0))


def _prompt_mixer_kernel(x_ref, g1_ref, w_in_ref, pool_w_ref, pool_scale_ref, cw_ref, cb_ref,
                         lru_w_ref, lru_b_ref, lam_ref, vg_ref, ws_ref, bs_ref,
                         w_pa_ref, w_pb_ref, w_pc_ref, w_o_ref,
                         xo_ref, pool_o_ref, rconv_o_ref, h_o_ref,
                         pool_prev, conv_prev, h_carry, kron_ref,
                         *, batch_major_in):
    t = pl.program_id(0)

    @pl.when(t == 0)
    def _():
        pool_prev[...] = jnp.zeros_like(pool_prev)
        conv_prev[...] = jnp.zeros_like(conv_prev)
        h_carry[...] = jnp.zeros_like(h_carry)
        _build_time_major_spatial_weights(ws_ref, kron_ref)

    dscale = _lru_log_decay_scale(lam_ref[...])
    g1 = g1_ref[...]
    pool_pre = pool_prev[...]
    conv_pre = conv_prev[...]
    h = h_carry[...]
    npre = pool_pre.shape[0]
    v_parts = []

    def load_normed(k):
        if batch_major_in:
            x = pltpu.einshape("sjc->jsc", x_ref[:, k * SUB_T:(k + 1) * SUB_T, :]).reshape(SUB_R, D_MODEL)
        else:
            x = x_ref[k * SUB_R:(k + 1) * SUB_R, :]
        return x, _bf(_rms(x, g1))

    n_sub = TILE_T // SUB_T
    x, xn = load_normed(0)
    a_in = _dot(xn, w_in_ref[:, C_POOL:C_POOL + D_POOL])
    b_x = _dot(xn, w_in_ref[:, C_BX:C_BX + D_RNN])
    for k in range(n_sub):
        r0 = k * SUB_R
        proj = lambda c0, n, xn=xn: _dot(xn, w_in_ref[:, c0:c0 + n])

        a_ext = jnp.concatenate([pool_pre, a_in], axis=0)
        pos = t * TILE_T + k * SUB_T + lax.shift_right_logical(
            lax.broadcasted_iota(jnp.int32, (SUB_R, POOL_GW), 0), PB_SHIFT)
        d_parts = []
        for g, w in enumerate(POOL_WINDOWS):
            cs = slice(g * POOL_GW, (g + 1) * POOL_GW)
            s = a_ext[npre - (w - 1) * PB:, cs]
            sh = 1
            while sh < w:
                s = s[sh * PB:] + s[:-sh * PB]
                sh *= 2
            cnt = jnp.minimum(pos + 1, w).astype(jnp.float32)
            d_parts.append(s / cnt - a_in[:, cs])
        pool_pre = a_ext[SUB_R:]
        pa = _pool_project(d_parts, pool_w_ref, pool_scale_ref, w_pa_ref)

        b_ext = jnp.concatenate([conv_pre, b_x], axis=0)
        bc = cb_ref[...] + cw_ref[CONV_WIDTH - 1:CONV_WIDTH, :] * b_x
        for m in range(1, CONV_WIDTH):
            lo = (CONV_WIDTH - 1 - m) * PB
            bc = bc + cw_ref[CONV_WIDTH - 1 - m:CONV_WIDTH - m, :] * b_ext[lo:lo + SUB_R]
        conv_pre = b_ext[SUB_R:]
        late = {}
        late_cols = {0: ('bg', C_BG, D_RNN), 1: ('cu', C_CU, D_CHUNK), 2: ('cv', C_CV, D_CHUNK),
                     3: ('ga', C_GA, D_MODEL), 4: ('gb', C_GB, D_MODEL), 5: ('gc', C_GC, D_MODEL)}

        def issue_late(hd):
            if hd in late_cols:
                name, c0, n = late_cols[hd]
                late[name] = proj(c0, n)

        a_dec, b_inp = _lru_coeffs(bc, lru_w_ref, lru_b_ref, dscale, after_head=issue_late)
        hs = []
        for j in range(SUB_T):
            h = a_dec[j * PB:(j + 1) * PB] * h + b_inp[j * PB:(j + 1) * PB]
            hs.append(h)
        pb = _dot(_bf(_gelu(late['bg']) * jnp.concatenate(hs, axis=0)), w_pb_ref[...])

        u = _gelu(late['cu'])
        v = _rms(_gelu(late['cv']), vg_ref[...])
        v_parts.append(_bf(v))
        v_seen = jnp.concatenate(v_parts, axis=0)
        mix = []
        for g in range(CHUNK_GROUPS):
            cs = slice(g * CHUNK_GW, (g + 1) * CHUNK_GW)
            m = _dot(kron_ref[g, r0:r0 + SUB_R, 0:r0 + SUB_R], v_seen[:, cs])
            mix.append(m + bs_ref[r0:r0 + SUB_R, g:g + 1])
        pc = _dot(_bf(u * jnp.concatenate(mix, axis=1)), w_pc_ref[...])

        more = k + 1 < n_sub
        if more:
            x_nxt, xn_nxt = load_normed(k + 1)
        merged = _sigmoid(late['ga']) * pa
        if more:
            a_nxt = _dot(xn_nxt, w_in_ref[:, C_POOL:C_POOL + D_POOL])
        merged = merged + _sigmoid(late['gb']) * pb
        if more:
            b_nxt = _dot(xn_nxt, w_in_ref[:, C_BX:C_BX + D_RNN])
        merged = merged + _sigmoid(late['gc']) * pc
        xo_ref[r0:r0 + SUB_R, :] = x + _dot(_bf(merged), w_o_ref[...])
        if more:
            x, xn, a_in, b_x = x_nxt, xn_nxt, a_nxt, b_nxt

    pool_prev[...] = pool_pre
    conv_prev[...] = conv_pre
    h_carry[...] = h
    pool_o_ref[...] = pool_pre
    rconv_o_ref[...] = conv_pre
    h_o_ref[...] = h


def _layer_spec(a, l):
    nd = a.ndim
    return pl.BlockSpec((None,) + a.shape[1:], lambda *_: (l,) + (0,) * (nd - 1),
                        pipeline_mode=pl.Buffered(1))


def _whole_spec(a):
    nd = a.ndim
    return pl.BlockSpec(a.shape, lambda *_: (0,) * nd, pipeline_mode=pl.Buffered(1))


_MIXER_PARAMS = ('g1', 'w_in', 'pool_w', 'pool_scale', 'cw', 'cb', 'lru_w', 'lru_b', 'lam', 'vg')
_PROJ_PARAMS = ('w_pa', 'w_pb', 'w_pc', 'w_o')
_FFN_PARAMS = ('g2', 'wg', 'wu', 'fcw', 'fcb', 'wd')


def _prompt_mixer(x, w, l, batch_major_in):
    if batch_major_in:
        n_rows = x.shape[0] * x.shape[1]
        x_spec = pl.BlockSpec((PB, TILE_T, D_MODEL), lambda t: (0, t, 0))
    else:
        n_rows = x.shape[0]
        x_spec = pl.BlockSpec((TILE_R, D_MODEL), lambda t: (t, 0))
    weights = [w[k] for k in _MIXER_PARAMS + ('ws', 'bs_tm') + _PROJ_PARAMS]
    pool_rows = (POOL_STATE + 1) * PB
    conv_rows = (CONV_WIDTH - 1) * PB
    f32 = jnp.float32
    return pl.pallas_call(
        functools.partial(_prompt_mixer_kernel, batch_major_in=batch_major_in),
        grid=(n_rows // TILE_R,),
        in_specs=[x_spec] + [_layer_spec(a, l) for a in weights],
        out_specs=[pl.BlockSpec((TILE_R, D_MODEL), lambda t: (t, 0)),
                   pl.BlockSpec((pool_rows, D_POOL), lambda t: (0, 0)),
                   pl.BlockSpec((conv_rows, D_RNN), lambda t: (0, 0)),
                   pl.BlockSpec((PB, D_RNN), lambda t: (0, 0))],
        out_shape=[jax.ShapeDtypeStruct((n_rows, D_MODEL), f32),
                   jax.ShapeDtypeStruct((pool_rows, D_POOL), f32),
                   jax.ShapeDtypeStruct((conv_rows, D_RNN), f32),
                   jax.ShapeDtypeStruct((PB, D_RNN), f32)],
        scratch_shapes=[pltpu.VMEM((pool_rows, D_POOL), f32),
                        pltpu.VMEM((conv_rows, D_RNN), f32),
                        pltpu.VMEM((PB, D_RNN), f32),
                        pltpu.VMEM((CHUNK_GROUPS, TILE_R, TILE_R), jnp.bfloat16)],
        compiler_params=pltpu.CompilerParams(dimension_semantics=("arbitrary",),
                                             vmem_limit_bytes=VMEM_LIMIT),
        name="prompt_mixer",
    )(x, *weights)


def _prompt_ffn_kernel(x_ref, g2_ref, wg_ref, wu_ref, fcw_ref, fcb_ref, wd_ref, gf_ref,
                       xo_ref, ffn_o_ref, g_prev, *, last_layer):
    t = pl.program_id(0)

    @pl.when(t == 0)
    def _():
        g_prev[...] = jnp.zeros_like(g_prev)

    x = x_ref[...]
    xn = _bf(_rms(x, g2_ref[...]))
    acc = x
    npre = (FFN_CONV - 1) * PB
    n_chunks = D_FF // FFN_C
    col = lambda c: slice(c * FFN_C, (c + 1) * FFN_C)
    up_proj = lambda c: (_dot(xn, wg_ref[:, col(c)]), _dot(xn, wu_ref[:, col(c)]))
    nxt = up_proj(0)
    for c in range(n_chunks):
        cs = col(c)
        g_pre, up = nxt
        if c + 1 < n_chunks:
            nxt = up_proj(c + 1)
        g_ext = jnp.concatenate([g_prev[:, cs], g_pre], axis=0)
        y = fcb_ref[:, cs] + fcw_ref[FFN_CONV - 1:FFN_CONV, cs] * g_pre
        for m in range(1, FFN_CONV):
            lo = npre - m * PB
            y = y + fcw_ref[FFN_CONV - 1 - m:FFN_CONV - m, cs] * g_ext[lo:lo + FFN_R]
        g_prev[:, cs] = g_ext[FFN_R:]
        h = _gelu(y) * up
        acc = acc + _dot(_bf(h), wd_ref[cs, :])
    if last_layer:
        y = _rms(acc, gf_ref[...]).reshape(FFN_T, PB, D_MODEL)
        xo_ref[...] = pltpu.einshape("jsc->sjc", y)
    else:
        xo_ref[...] = acc
    ffn_o_ref[...] = g_prev[...]


def _prompt_ffn(x_tm, w, l, last_layer):
    n_rows = x_tm.shape[0]
    row_spec = pl.BlockSpec((FFN_R, D_MODEL), lambda t: (t, 0))
    pre_rows = (FFN_CONV - 1) * PB
    if last_layer:
        out_spec = pl.BlockSpec((PB, FFN_T, D_MODEL), lambda t: (0, t, 0))
        out_shape = jax.ShapeDtypeStruct((PB, n_rows // PB, D_MODEL), jnp.float32)
    else:
        out_spec, out_shape = row_spec, jax.ShapeDtypeStruct((n_rows, D_MODEL), jnp.float32)
    weights = [w[k] for k in _FFN_PARAMS]
    return pl.pallas_call(
        functools.partial(_prompt_ffn_kernel, last_layer=last_layer),
        grid=(n_rows // FFN_R,),
        in_specs=[row_spec] + [_layer_spec(a, l) for a in weights] + [_whole_spec(w['gf'])],
        out_specs=[out_spec, pl.BlockSpec((pre_rows, D_FF), lambda t: (0, 0))],
        out_shape=[out_shape, jax.ShapeDtypeStruct((pre_rows, D_FF), jnp.float32)],
        scratch_shapes=[pltpu.VMEM((pre_rows, D_FF), jnp.float32)],
        compiler_params=pltpu.CompilerParams(dimension_semantics=("arbitrary",),
                                             vmem_limit_bytes=VMEM_LIMIT),
        name="prompt_ffn",
    )(x_tm, *weights, w['gf'])


def _sample_mixer_kernel(x_ref, pool_st_ref, rconv_st_ref, h0_ref,
                         g1_ref, w_in_ref, pool_w_ref, pool_scale_ref, cw_ref, cb_ref,
                         lru_w_ref, lru_b_ref, lam_ref, vg_ref, ws0_ref, bs0_ref,
                         w_pa_ref, w_pb_ref, w_pc_ref, w_o_ref,
                         xo_ref, pool_o_ref, rconv_o_ref, h_o_ref, v_o_ref):
    x = x_ref[...]
    xn = _bf(_rms(x, g1_ref[...]))

    a_in = _dot(xn, w_in_ref[:, C_POOL:C_POOL + D_POOL])
    d_parts = []
    for g, w in enumerate(POOL_WINDOWS):
        s = a_in[:, g * POOL_GW:(g + 1) * POOL_GW]
        for m in range(1, w):
            s = s + pool_st_ref[POOL_STATE - m, :, g * POOL_GW:(g + 1) * POOL_GW]
        cnt = float(min(PAST_LEN + 1, w))
        d_parts.append(s / cnt - a_in[:, g * POOL_GW:(g + 1) * POOL_GW])
    pool_o_ref[:POOL_STATE - 1] = pool_st_ref[1:]
    pool_o_ref[POOL_STATE - 1] = a_in
    pa = _pool_project(d_parts, pool_w_ref, pool_scale_ref, w_pa_ref)

    b_x = _dot(xn, w_in_ref[:, C_BX:C_BX + D_RNN])
    bc = cb_ref[...] + cw_ref[CONV_WIDTH - 1:CONV_WIDTH, :] * b_x
    for m in range(CONV_WIDTH - 1):
        bc = bc + cw_ref[m:m + 1, :] * rconv_st_ref[m]
    rconv_o_ref[:CONV_WIDTH - 2] = rconv_st_ref[1:]
    rconv_o_ref[CONV_WIDTH - 2] = b_x
    a_dec, b_inp = _lru_coeffs(bc, lru_w_ref, lru_b_ref, _lru_log_decay_scale(lam_ref[...]))
    h = a_dec * h0_ref[...] + b_inp
    h_o_ref[...] = h
    b_gate = _dot(xn, w_in_ref[:, C_BG:C_BG + D_RNN])
    pb = _dot(_bf(_gelu(b_gate) * h), w_pb_ref[...])

    u = _gelu(_dot(xn, w_in_ref[:, C_CU:C_CU + D_CHUNK]))
    v = _rms(_gelu(_dot(xn, w_in_ref[:, C_CV:C_CV + D_CHUNK])), vg_ref[...])
    v_o_ref[...] = v
    mix = ws0_ref[...] * v + bs0_ref[...]
    pc = _dot(_bf(u * mix), w_pc_ref[...])

    merged = _sigmoid(_dot(xn, w_in_ref[:, C_GA:C_GA + D_MODEL])) * pa
    merged = merged + _sigmoid(_dot(xn, w_in_ref[:, C_GB:C_GB + D_MODEL])) * pb
    merged = merged + _sigmoid(_dot(xn, w_in_ref[:, C_GC:C_GC + D_MODEL])) * pc
    xo_ref[...] = x + _dot(_bf(merged), w_o_ref[...])


def _sample_mixer(xs, pool_st, rconv_st, h0, w, l):
    n = xs.shape[0]
    f32 = jnp.float32
    acts = [xs, pool_st, rconv_st, h0]
    weights = [w[k] for k in _MIXER_PARAMS + ('ws0', 'bs0') + _PROJ_PARAMS]
    out_shape = [jax.ShapeDtypeStruct((n, D_MODEL), f32),
                 jax.ShapeDtypeStruct((POOL_STATE, n, D_POOL), f32),
                 jax.ShapeDtypeStruct((CONV_WIDTH - 1, n, D_RNN), f32),
                 jax.ShapeDtypeStruct((n, D_RNN), f32),
                 jax.ShapeDtypeStruct((n, D_CHUNK), f32)]
    return pl.pallas_call(
        _sample_mixer_kernel,
        grid=(1,),
        in_specs=[_whole_spec(xs)] + [_layer_spec(a, l) for a in acts[1:] + weights],
        out_specs=[pl.BlockSpec(s.shape, lambda t, nd=len(s.shape): (0,) * nd) for s in out_shape],
        out_shape=out_shape,
        compiler_params=pltpu.CompilerParams(dimension_semantics=("arbitrary",),
                                             vmem_limit_bytes=VMEM_LIMIT),
        name="sample_mixer",
    )(*acts, *weights)


def _sample_ffn_kernel(x_ref, st_ref, g2_ref, wg_ref, wu_ref, fcw_ref, fcb_ref, wd_ref, gf_ref,
                       xo_ref, st_o_ref, *, last_layer):
    x = x_ref[...]
    xn = _bf(_rms(x, g2_ref[...]))
    g_pre = _dot(xn, wg_ref[...])
    y = fcb_ref[...] + fcw_ref[FFN_CONV - 1:FFN_CONV, :] * g_pre
    for m in range(FFN_CONV - 1):
        y = y + fcw_ref[m:m + 1, :] * st_ref[m]
    st_o_ref[:FFN_CONV - 2] = st_ref[1:]
    st_o_ref[FFN_CONV - 2] = g_pre
    h = _gelu(y) * _dot(xn, wu_ref[...])
    out = x + _dot(_bf(h), wd_ref[...])
    if last_layer:
        out = _rms(out, gf_ref[...])
    xo_ref[...] = out


def _sample_ffn(xs, st, w, l, last_layer):
    n = xs.shape[0]
    weights = [w[k] for k in _FFN_PARAMS]
    out_shape = [jax.ShapeDtypeStruct((n, D_MODEL), jnp.float32),
                 jax.ShapeDtypeStruct((FFN_CONV - 1, n, D_FF), jnp.float32)]
    return pl.pallas_call(
        functools.partial(_sample_ffn_kernel, last_layer=last_layer),
        grid=(1,),
        in_specs=[_whole_spec(xs), _layer_spec(st, l)] + [_layer_spec(a, l) for a in weights]
                 + [_whole_spec(w['gf'])],
        out_specs=[pl.BlockSpec(s.shape, lambda t, nd=len(s.shape): (0,) * nd) for s in out_shape],
        out_shape=out_shape,
        compiler_params=pltpu.CompilerParams(dimension_semantics=("arbitrary",),
                                             vmem_limit_bytes=VMEM_LIMIT),
        name="sample_ffn",
    )(xs, st, *weights, w['gf'])


def _prepare_weights(p):
    bf16 = jnp.bfloat16
    depth = p['w_in'].shape[0]
    row = lambda a: a.reshape(depth, 1, -1)
    return {
        'g1': row(p['norm1_g']), 'w_in': p['w_in'].astype(bf16),
        'pool_w': p['pool_w'].astype(bf16), 'pool_scale': row(p['pool_scale']),
        'cw': p['rnn_conv_w'], 'cb': row(p['rnn_conv_b']),
        'lru_w': jnp.concatenate([p['lru_wa'], p['lru_wx']], axis=-1).astype(bf16),
        'lru_b': jnp.stack([p['lru_ba'], p['lru_bx']], axis=1),
        'lam': row(p['lru_lambda']), 'vg': row(p['chunk_vnorm_g']),
        'ws': p['chunk_ws'].astype(bf16),
        'bs_tm': jnp.repeat(jnp.swapaxes(p['chunk_bs'], 1, 2), PB, axis=1),
        'ws0': row(jnp.repeat(p['chunk_ws'][:, :, 0, 0], CHUNK_GW, axis=1)),
        'bs0': row(jnp.repeat(p['chunk_bs'][:, :, 0], CHUNK_GW, axis=1)),
        'w_pa': p['w_pa'].astype(bf16), 'w_pb': p['w_pb'].astype(bf16),
        'w_pc': p['w_pc'].astype(bf16), 'w_o': p['w_o'].astype(bf16),
        'g2': row(p['norm2_g']), 'wg': p['ffn_wg'].astype(bf16),
        'wu': p['ffn_wu'].astype(bf16), 'fcw': p['ffn_conv_w'],
        'fcb': row(p['ffn_conv_b']), 'wd': p['ffn_wd'].astype(bf16),
        'gf': p['final_norm_g'].reshape(1, -1),
    }


def _batch_major(rows_tm, n):
    return jnp.transpose(rows_tm.reshape(-1, PB, rows_tm.shape[-1])[-n:], (1, 0, 2))


def kernel(x_prompt, x_sample, state_pool, state_rnn_conv, state_rnn_h, state_ffn_conv, norm1_g, w_in, pool_w, pool_scale, rnn_conv_w, rnn_conv_b, lru_wa, lru_ba, lru_wx, lru_bx, lru_lambda, chunk_vnorm_g, chunk_ws, chunk_bs, w_pa, w_pb, w_pc, w_o, norm2_g, ffn_wg, ffn_wu, ffn_conv_w, ffn_conv_b, ffn_wd, final_norm_g):
    p = dict(norm1_g=norm1_g, w_in=w_in, pool_w=pool_w, pool_scale=pool_scale,
             rnn_conv_w=rnn_conv_w, rnn_conv_b=rnn_conv_b, lru_wa=lru_wa, lru_ba=lru_ba,
             lru_wx=lru_wx, lru_bx=lru_bx, lru_lambda=lru_lambda, chunk_vnorm_g=chunk_vnorm_g,
             chunk_ws=chunk_ws, chunk_bs=chunk_bs, w_pa=w_pa, w_pb=w_pb, w_pc=w_pc, w_o=w_o,
             norm2_g=norm2_g, ffn_wg=ffn_wg, ffn_wu=ffn_wu, ffn_conv_w=ffn_conv_w,
             ffn_conv_b=ffn_conv_b, ffn_wd=ffn_wd, final_norm_g=final_norm_g)
    depth = w_in.shape[0]
    bp, tp, _ = x_prompt.shape
    ns = x_sample.shape[0]
    assert bp == PB and tp % TILE_T == 0 and x_sample.shape[1] == 1
    w = _prepare_weights(p)

    xp = x_prompt
    xs = x_sample.reshape(ns, D_MODEL)
    pool_st = jnp.swapaxes(state_pool, 1, 2)
    rconv_st = jnp.swapaxes(state_rnn_conv, 1, 2)
    ffn_st = jnp.swapaxes(state_ffn_conv, 1, 2)
    outs = {k: [] for k in ('pool_p', 'pool_s', 'rc_p', 'rc_s', 'h_p', 'h_s', 'ff_p', 'ff_s', 'cv_s')}
    for l in range(depth):
        last = l == depth - 1
        xp, pool_tm, rconv_tm, h_p = _prompt_mixer(xp, w, l, batch_major_in=(l == 0))
        xp, ffn_tm = _prompt_ffn(xp, w, l, last)
        xs, pool_s, rconv_s, h_s, v_s = _sample_mixer(xs, pool_st, rconv_st, state_rnn_h, w, l)
        xs, ffn_s = _sample_ffn(xs, ffn_st, w, l, last)
        outs['pool_p'].append(_batch_major(pool_tm, POOL_STATE))
        outs['rc_p'].append(_batch_major(rconv_tm, CONV_WIDTH - 1))
        outs['h_p'].append(h_p)
        outs['ff_p'].append(_batch_major(ffn_tm, FFN_CONV - 1))
        outs['pool_s'].append(jnp.swapaxes(pool_s, 0, 1))
        outs['rc_s'].append(jnp.swapaxes(rconv_s, 0, 1))
        outs['h_s'].append(h_s)
        outs['ff_s'].append(jnp.swapaxes(ffn_s, 0, 1))
        outs['cv_s'].append(v_s.reshape(ns, 1, D_CHUNK))
    st = jnp.stack
    return (xp, xs.reshape(ns, 1, D_MODEL), st(outs['pool_p']), st(outs['pool_s']),
            st(outs['rc_p']), st(outs['rc_s']), st(outs['h_p']), st(outs['h_s']),
            st(outs['ff_p']), st(outs['ff_s']), st(outs['cv_s']))
```

```python
import functools
import math

import jax
import jax.numpy as jnp
from jax import lax
from jax.experimental import pallas as pl
from jax.experimental.pallas import tpu as pltpu

D_MODEL = 1024
D_POOL = 512
POOL_WINDOWS = (2, 4, 8, 16)
POOL_GW = 128
POOL_STATE = 15
D_RNN = 1024
RNN_HEADS = 8
RNN_HD = 128
CONV_WIDTH = 4
LRU_C = 8.0
D_CHUNK = 512
CHUNK = 128
CHUNK_GROUPS = 4
CHUNK_GW = 128
D_FF = 3072
FFN_CONV = 3
EPS = 1e-6
PAST_LEN = 16384

C_POOL = 0
C_BX = D_POOL
C_BG = C_BX + D_RNN
C_CU = C_BG + D_RNN
C_CV = C_CU + D_CHUNK
C_GA = C_CV + D_CHUNK
C_GB = C_GA + D_MODEL
C_GC = C_GB + D_MODEL

PB = 8
PB_SHIFT = PB.bit_length() - 1
TILE_T = CHUNK
TILE_R = TILE_T * PB
SUB_T = 64
SUB_R = SUB_T * PB
FFN_R = 512
FFN_T = FFN_R // PB
FFN_C = 1536
VMEM_LIMIT = 60 * 1024 * 1024

_GELU_K = math.sqrt(2.0 / math.pi)


def _dot(a, b):
    return jnp.dot(a, b, preferred_element_type=jnp.float32)


def _bf(x):
    return x.astype(jnp.bfloat16)


def _gelu(x):
    return 0.5 * x * (1.0 + jnp.tanh(_GELU_K * (x + 0.044715 * (x * x * x))))


def _sigmoid(x):
    return 0.5 * (jnp.tanh(0.5 * x) + 1.0)


def _rms(x, g):
    return x * lax.rsqrt(jnp.mean(x * x, axis=-1, keepdims=True) + EPS) * g


def _lru_log_decay_scale(lam):
    z = -lam
    sp = jnp.maximum(z, 0.0) + jnp.log1p(jnp.exp(-jnp.abs(z)))
    return -LRU_C * sp


def _lru_coeffs(bc, lru_w_ref, lru_b_ref, dscale, after_head=None):
    a_parts, b_parts = [], []
    for h in range(RNN_HEADS):
        if after_head is not None and h > 0:
            after_head(h - 1)
        cs = slice(h * RNN_HD, (h + 1) * RNN_HD)
        bch = bc[:, cs]
        ri = _dot(_bf(bch), lru_w_ref[h])
        r = _sigmoid(ri[:, :RNN_HD] + lru_b_ref[0:1, cs])
        i = _sigmoid(ri[:, RNN_HD:] + lru_b_ref[1:2, cs])
        log_a = dscale[:, cs] * r
        a = jnp.exp(log_a)
        a_parts.append(a)
        b_parts.append(jnp.sqrt((1.0 - a) * (1.0 + a)) * (i * bch))
    return jnp.concatenate(a_parts, axis=1), jnp.concatenate(b_parts, axis=1)


def _pool_project(d_parts, pool_w_ref, pool_scale_ref, w_pa_ref):
    ya = [_dot(_bf(d), pool_w_ref[g]) * pool_scale_ref[:, g * POOL_GW:(g + 1) * POOL_GW]
          for g, d in enumerate(d_parts)]
    return _dot(_bf(jnp.concatenate(ya, axis=1)), w_pa_ref[...])


def _load_weight_as_bf16(src_hbm, layer, dst_ref, rows_per_chunk):
    n_rows, n_cols = dst_ref.shape
    assert n_rows % rows_per_chunk == 0
    n_chunks = n_rows // rows_per_chunk

    def body(stage, sem):
        def copy(i):
            rows = pl.ds(i * rows_per_chunk, rows_per_chunk)
            return pltpu.make_async_copy(src_hbm.at[layer, rows, :], stage.at[i % 2], sem.at[i % 2])

        copy(0).start()
        for i in range(n_chunks):
            if i + 1 < n_chunks:
                copy(i + 1).start()
            copy(i).wait()
            dst_ref[i * rows_per_chunk:(i + 1) * rows_per_chunk, :] = _bf(stage[i % 2])

    pl.run_scoped(body, pltpu.VMEM((2, rows_per_chunk, n_cols), jnp.float32),
                  pltpu.SemaphoreType.DMA((2,)))


def _publish_copies(pairs, sem):
    return [pltpu.make_async_copy(src, dst, sem.at[i]) for i, (src, dst) in enumerate(pairs)]


def _build_time_major_spatial_weights(ws_ref, kron_refs):
    row_t = lax.shift_right_logical(lax.broadcasted_iota(jnp.int32, (SUB_R, SUB_T), 0), PB_SHIFT)
    repeat_rows = _bf(row_t == lax.broadcasted_iota(jnp.int32, (SUB_R, SUB_T), 1))
    for k, kron_ref in enumerate(kron_refs):
        n_t, n_r = (k + 1) * SUB_T, (k + 1) * SUB_R
        col_t = lax.shift_right_logical(lax.broadcasted_iota(jnp.int32, (n_t, n_r), 1), PB_SHIFT)
        repeat_cols = _bf(col_t == lax.broadcasted_iota(jnp.int32, (n_t, n_r), 0))
        rr = lax.broadcasted_iota(jnp.int32, (SUB_R, n_r), 0)
        cc = lax.broadcasted_iota(jnp.int32, (SUB_R, n_r), 1)
        same_seq = (rr & (PB - 1)) == (cc & (PB - 1))
        causal = (lax.shift_right_logical(rr, PB_SHIFT) + k * SUB_T) >= lax.shift_right_logical(cc, PB_SHIFT)
        keep = same_seq & causal
        for g in range(CHUNK_GROUPS):
            rows = _dot(repeat_rows, ws_ref[g, k * SUB_T:(k + 1) * SUB_T, 0:n_t])
            full = _dot(_bf(rows), repeat_cols)
            kron_ref[g] = _bf(jnp.where(keep, full, 0.0))


def _prompt_mixer_kernel(x_ref, g1_ref, pool_w_ref, pool_scale_ref, cw_ref, cb_ref,
                         lru_w_ref, lru_b_ref, lam_ref, vg_ref, ws_ref, bs_ref,
                         w_in_hbm, w_pa_hbm, w_pb_hbm, w_pc_hbm, w_o_hbm,
                         xo_ref, pool_o_ref, rconv_o_ref, h_o_ref,
                         w_in_pub, w_pa_pub, w_pb_pub, w_pc_pub, w_o_pub,
                         pool_prev, conv_prev, h_carry,
                         w_in_ref, w_pa_ref, w_pb_ref, w_pc_ref, w_o_ref, pub_sem, *kron_refs,
                         layer, batch_major_in):
    t = pl.program_id(0)
    publish = _publish_copies([(w_in_ref, w_in_pub), (w_pa_ref, w_pa_pub), (w_pb_ref, w_pb_pub),
                               (w_pc_ref, w_pc_pub), (w_o_ref, w_o_pub)], pub_sem)

    @pl.when(t == 0)
    def _():
        pool_prev[...] = jnp.zeros_like(pool_prev)
        conv_prev[...] = jnp.zeros_like(conv_prev)
        h_carry[...] = jnp.zeros_like(h_carry)
        _load_weight_as_bf16(w_in_hbm, layer, w_in_ref, 32)
        for src, dst in ((w_pa_hbm, w_pa_ref), (w_pb_hbm, w_pb_ref), (w_pc_hbm, w_pc_ref),
                         (w_o_hbm, w_o_ref)):
            _load_weight_as_bf16(src, layer, dst, 128)
        for c in publish:
            c.start()
        _build_time_major_spatial_weights(ws_ref, kron_refs)

    @pl.when(t == pl.num_programs(0) - 1)
    def _():
        for c in publish:
            c.wait()

    dscale = _lru_log_decay_scale(lam_ref[...])
    g1 = g1_ref[...]
    pool_pre = pool_prev[...]
    conv_pre = conv_prev[...]
    h = h_carry[...]
    npre = pool_pre.shape[0]
    v_parts = []

    def load_normed(k):
        if batch_major_in:
            x = pltpu.einshape("sjc->jsc", x_ref[:, k * SUB_T:(k + 1) * SUB_T, :]).reshape(SUB_R, D_MODEL)
        else:
            x = x_ref[k * SUB_R:(k + 1) * SUB_R, :]
        return x, _bf(_rms(x, g1))

    n_sub = TILE_T // SUB_T
    x, xn = load_normed(0)
    a_in = _dot(xn, w_in_ref[:, C_POOL:C_POOL + D_POOL])
    b_x = _dot(xn, w_in_ref[:, C_BX:C_BX + D_RNN])
    for k in range(n_sub):
        r0 = k * SUB_R
        proj = lambda c0, n, xn=xn: _dot(xn, w_in_ref[:, c0:c0 + n])

        a_ext = jnp.concatenate([pool_pre, a_in], axis=0)
        pos = t * TILE_T + k * SUB_T + lax.shift_right_logical(
            lax.broadcasted_iota(jnp.int32, (SUB_R, POOL_GW), 0), PB_SHIFT)
        d_parts = []
        for g, w in enumerate(POOL_WINDOWS):
            cs = slice(g * POOL_GW, (g + 1) * POOL_GW)
            s = a_ext[npre - (w - 1) * PB:, cs]
            sh = 1
            while sh < w:
                s = s[sh * PB:] + s[:-sh * PB]
                sh *= 2
            cnt = jnp.minimum(pos + 1, w).astype(jnp.float32)
            d_parts.append(s / cnt - a_in[:, cs])
        pool_pre = a_ext[SUB_R:]
        pa = _pool_project(d_parts, pool_w_ref, pool_scale_ref, w_pa_ref)

        b_ext = jnp.concatenate([conv_pre, b_x], axis=0)
        bc = cb_ref[...] + cw_ref[CONV_WIDTH - 1:CONV_WIDTH, :] * b_x
        for m in range(1, CONV_WIDTH):
            lo = (CONV_WIDTH - 1 - m) * PB
            bc = bc + cw_ref[CONV_WIDTH - 1 - m:CONV_WIDTH - m, :] * b_ext[lo:lo + SUB_R]
        conv_pre = b_ext[SUB_R:]
        late = {}
        late_cols = {0: ('bg', C_BG, D_RNN), 1: ('cu', C_CU, D_CHUNK), 2: ('cv', C_CV, D_CHUNK),
                     3: ('ga', C_GA, D_MODEL), 4: ('gb', C_GB, D_MODEL), 5: ('gc', C_GC, D_MODEL)}

        def issue_late(hd):
            if hd in late_cols:
                name, c0, n = late_cols[hd]
                late[name] = proj(c0, n)

        a_dec, b_inp = _lru_coeffs(bc, lru_w_ref, lru_b_ref, dscale, after_head=issue_late)
        hs = []
        for j in range(SUB_T):
            h = a_dec[j * PB:(j + 1) * PB] * h + b_inp[j * PB:(j + 1) * PB]
            hs.append(h)
        pb = _dot(_bf(_gelu(late['bg']) * jnp.concatenate(hs, axis=0)), w_pb_ref[...])

        u = _gelu(late['cu'])
        v = _rms(_gelu(late['cv']), vg_ref[...])
        v_parts.append(_bf(v))
        v_seen = jnp.concatenate(v_parts, axis=0)
        mix = []
        for g in range(CHUNK_GROUPS):
            cs = slice(g * CHUNK_GW, (g + 1) * CHUNK_GW)
            m = _dot(kron_refs[k][g], v_seen[:, cs])
            mix.append(m + bs_ref[r0:r0 + SUB_R, g:g + 1])
        pc = _dot(_bf(u * jnp.concatenate(mix, axis=1)), w_pc_ref[...])

        more = k + 1 < n_sub
        if more:
            x_nxt, xn_nxt = load_normed(k + 1)
        merged = _sigmoid(late['ga']) * pa
        if more:
            a_nxt = _dot(xn_nxt, w_in_ref[:, C_POOL:C_POOL + D_POOL])
        merged = merged + _sigmoid(late['gb']) * pb
        if more:
            b_nxt = _dot(xn_nxt, w_in_ref[:, C_BX:C_BX + D_RNN])
        merged = merged + _sigmoid(late['gc']) * pc
        xo_ref[r0:r0 + SUB_R, :] = x + _dot(_bf(merged), w_o_ref[...])
        if more:
            x, xn, a_in, b_x = x_nxt, xn_nxt, a_nxt, b_nxt

    pool_prev[...] = pool_pre
    conv_prev[...] = conv_pre
    h_carry[...] = h
    pool_o_ref[...] = pool_pre
    rconv_o_ref[...] = conv_pre
    h_o_ref[...] = h


def _layer_spec(a, l):
    nd = a.ndim
    return pl.BlockSpec((None,) + a.shape[1:], lambda *_: (l,) + (0,) * (nd - 1),
                        pipeline_mode=pl.Buffered(1))


def _whole_spec(a):
    nd = a.ndim
    return pl.BlockSpec(a.shape, lambda *_: (0,) * nd, pipeline_mode=pl.Buffered(1))


_MIXER_SMALL = ('g1', 'pool_w', 'pool_scale', 'cw', 'cb', 'lru_w', 'lru_b', 'lam', 'vg')
_MIXER_BIG = ('w_in', 'w_pa', 'w_pb', 'w_pc', 'w_o')
_FFN_SMALL = ('g2', 'fcw', 'fcb')
_FFN_BIG = ('wg', 'wu', 'wd')
_ANY = pl.BlockSpec(memory_space=pl.ANY)


def _prompt_mixer(x, w, l, batch_major_in):
    if batch_major_in:
        n_rows = x.shape[0] * x.shape[1]
        x_spec = pl.BlockSpec((PB, TILE_T, D_MODEL), lambda t: (0, t, 0))
    else:
        n_rows = x.shape[0]
        x_spec = pl.BlockSpec((TILE_R, D_MODEL), lambda t: (t, 0))
    small = [w[k] for k in _MIXER_SMALL + ('ws', 'bs_tm')]
    big = [w[k] for k in _MIXER_BIG]
    pool_rows = (POOL_STATE + 1) * PB
    conv_rows = (CONV_WIDTH - 1) * PB
    f32, bf16 = jnp.float32, jnp.bfloat16
    outs = pl.pallas_call(
        functools.partial(_prompt_mixer_kernel, layer=l, batch_major_in=batch_major_in),
        grid=(n_rows // TILE_R,),
        in_specs=[x_spec] + [_layer_spec(a, l) for a in small] + [_ANY] * len(big),
        out_specs=[pl.BlockSpec((TILE_R, D_MODEL), lambda t: (t, 0)),
                   pl.BlockSpec((pool_rows, D_POOL), lambda t: (0, 0)),
                   pl.BlockSpec((conv_rows, D_RNN), lambda t: (0, 0)),
                   pl.BlockSpec((PB, D_RNN), lambda t: (0, 0))] + [_ANY] * len(big),
        out_shape=[jax.ShapeDtypeStruct((n_rows, D_MODEL), f32),
                   jax.ShapeDtypeStruct((pool_rows, D_POOL), f32),
                   jax.ShapeDtypeStruct((conv_rows, D_RNN), f32),
                   jax.ShapeDtypeStruct((PB, D_RNN), f32)]
                  + [jax.ShapeDtypeStruct(a.shape[1:], bf16) for a in big],
        scratch_shapes=[pltpu.VMEM((pool_rows, D_POOL), f32),
                        pltpu.VMEM((conv_rows, D_RNN), f32),
                        pltpu.VMEM((PB, D_RNN), f32)]
                       + [pltpu.VMEM(a.shape[1:], bf16) for a in big]
                       + [pltpu.SemaphoreType.DMA((len(big),))]
                       + [pltpu.VMEM((CHUNK_GROUPS, SUB_R, (k + 1) * SUB_R), bf16)
                          for k in range(TILE_T // SUB_T)],
        compiler_params=pltpu.CompilerParams(dimension_semantics=("arbitrary",),
                                             vmem_limit_bytes=VMEM_LIMIT),
        name="prompt_mixer",
    )(x, *small, *big)
    return outs[:4], dict(zip(_MIXER_BIG, outs[4:]))


def _prompt_ffn_kernel(x_ref, g2_ref, fcw_ref, fcb_ref, gf_ref, wg_hbm, wu_hbm, wd_hbm,
                       xo_ref, ffn_o_ref, wg_pub, wu_pub, wd_pub,
                       g_prev, wg_ref, wu_ref, wd_ref, pub_sem, *, layer, last_layer):
    t = pl.program_id(0)
    publish = _publish_copies([(wg_ref, wg_pub), (wu_ref, wu_pub), (wd_ref, wd_pub)], pub_sem)

    @pl.when(t == 0)
    def _():
        g_prev[...] = jnp.zeros_like(g_prev)
        _load_weight_as_bf16(wg_hbm, layer, wg_ref, 128)
        _load_weight_as_bf16(wu_hbm, layer, wu_ref, 128)
        _load_weight_as_bf16(wd_hbm, layer, wd_ref, 256)
        for c in publish:
            c.start()

    @pl.when(t == pl.num_programs(0) - 1)
    def _():
        for c in publish:
            c.wait()

    x = x_ref[...]
    xn = _bf(_rms(x, g2_ref[...]))
    acc = x
    npre = (FFN_CONV - 1) * PB
    n_chunks = D_FF // FFN_C
    col = lambda c: slice(c * FFN_C, (c + 1) * FFN_C)
    up_proj = lambda c: (_dot(xn, wg_ref[:, col(c)]), _dot(xn, wu_ref[:, col(c)]))
    nxt = up_proj(0)
    for c in range(n_chunks):
        cs = col(c)
        g_pre, up = nxt
        if c + 1 < n_chunks:
            nxt = up_proj(c + 1)
        g_ext = jnp.concatenate([g_prev[:, cs], g_pre], axis=0)
        y = fcb_ref[:, cs] + fcw_ref[FFN_CONV - 1:FFN_CONV, cs] * g_pre
        for m in range(1, FFN_CONV):
            lo = npre - m * PB
            y = y + fcw_ref[FFN_CONV - 1 - m:FFN_CONV - m, cs] * g_ext[lo:lo + FFN_R]
        g_prev[:, cs] = g_ext[FFN_R:]
        h = _gelu(y) * up
        acc = acc + _dot(_bf(h), wd_ref[cs, :])
    if last_layer:
        y = _rms(acc, gf_ref[...]).reshape(FFN_T, PB, D_MODEL)
        xo_ref[...] = pltpu.einshape("jsc->sjc", y)
    else:
        xo_ref[...] = acc
    ffn_o_ref[...] = g_prev[...]


def _prompt_ffn(x_tm, w, l, last_layer):
    n_rows = x_tm.shape[0]
    row_spec = pl.BlockSpec((FFN_R, D_MODEL), lambda t: (t, 0))
    pre_rows = (FFN_CONV - 1) * PB
    if last_layer:
        out_spec = pl.BlockSpec((PB, FFN_T, D_MODEL), lambda t: (0, t, 0))
        out_shape = jax.ShapeDtypeStruct((PB, n_rows // PB, D_MODEL), jnp.float32)
    else:
        out_spec, out_shape = row_spec, jax.ShapeDtypeStruct((n_rows, D_MODEL), jnp.float32)
    small = [w[k] for k in _FFN_SMALL]
    big = [w[k] for k in _FFN_BIG]
    bf16 = jnp.bfloat16
    outs = pl.pallas_call(
        functools.partial(_prompt_ffn_kernel, layer=l, last_layer=last_layer),
        grid=(n_rows // FFN_R,),
        in_specs=[row_spec] + [_layer_spec(a, l) for a in small] + [_whole_spec(w['gf'])]
                 + [_ANY] * len(big),
        out_specs=[out_spec, pl.BlockSpec((pre_rows, D_FF), lambda t: (0, 0))] + [_ANY] * len(big),
        out_shape=[out_shape, jax.ShapeDtypeStruct((pre_rows, D_FF), jnp.float32)]
                  + [jax.ShapeDtypeStruct(a.shape[1:], bf16) for a in big],
        scratch_shapes=[pltpu.VMEM((pre_rows, D_FF), jnp.float32)]
                       + [pltpu.VMEM(a.shape[1:], bf16) for a in big]
                       + [pltpu.SemaphoreType.DMA((len(big),))],
        compiler_params=pltpu.CompilerParams(dimension_semantics=("arbitrary",),
                                             vmem_limit_bytes=VMEM_LIMIT),
        name="prompt_ffn",
    )(x_tm, *small, w['gf'], *big)
    return outs[:2], dict(zip(_FFN_BIG, outs[2:]))


def _sample_mixer_kernel(x_ref, pool_st_ref, rconv_st_ref, h0_ref,
                         g1_ref, pool_w_ref, pool_scale_ref, cw_ref, cb_ref,
                         lru_w_ref, lru_b_ref, lam_ref, vg_ref, ws0_ref, bs0_ref,
                         w_in_ref, w_pa_ref, w_pb_ref, w_pc_ref, w_o_ref,
                         xo_ref, pool_o_ref, rconv_o_ref, h_o_ref, v_o_ref):
    x = x_ref[...]
    xn = _bf(_rms(x, g1_ref[...]))

    a_in = _dot(xn, w_in_ref[:, C_POOL:C_POOL + D_POOL])
    d_parts = []
    for g, w in enumerate(POOL_WINDOWS):
        s = a_in[:, g * POOL_GW:(g + 1) * POOL_GW]
        for m in range(1, w):
            s = s + pool_st_ref[POOL_STATE - m, :, g * POOL_GW:(g + 1) * POOL_GW]
        cnt = float(min(PAST_LEN + 1, w))
        d_parts.append(s / cnt - a_in[:, g * POOL_GW:(g + 1) * POOL_GW])
    pool_o_ref[:POOL_STATE - 1] = pool_st_ref[1:]
    pool_o_ref[POOL_STATE - 1] = a_in
    pa = _pool_project(d_parts, pool_w_ref, pool_scale_ref, w_pa_ref)

    b_x = _dot(xn, w_in_ref[:, C_BX:C_BX + D_RNN])
    bc = cb_ref[...] + cw_ref[CONV_WIDTH - 1:CONV_WIDTH, :] * b_x
    for m in range(CONV_WIDTH - 1):
        bc = bc + cw_ref[m:m + 1, :] * rconv_st_ref[m]
    rconv_o_ref[:CONV_WIDTH - 2] = rconv_st_ref[1:]
    rconv_o_ref[CONV_WIDTH - 2] = b_x
    a_dec, b_inp = _lru_coeffs(bc, lru_w_ref, lru_b_ref, _lru_log_decay_scale(lam_ref[...]))
    h = a_dec * h0_ref[...] + b_inp
    h_o_ref[...] = h
    b_gate = _dot(xn, w_in_ref[:, C_BG:C_BG + D_RNN])
    pb = _dot(_bf(_gelu(b_gate) * h), w_pb_ref[...])

    u = _gelu(_dot(xn, w_in_ref[:, C_CU:C_CU + D_CHUNK]))
    v = _rms(_gelu(_dot(xn, w_in_ref[:, C_CV:C_CV + D_CHUNK])), vg_ref[...])
    v_o_ref[...] = v
    mix = ws0_ref[...] * v + bs0_ref[...]
    pc = _dot(_bf(u * mix), w_pc_ref[...])

    merged = _sigmoid(_dot(xn, w_in_ref[:, C_GA:C_GA + D_MODEL])) * pa
    merged = merged + _sigmoid(_dot(xn, w_in_ref[:, C_GB:C_GB + D_MODEL])) * pb
    merged = merged + _sigmoid(_dot(xn, w_in_ref[:, C_GC:C_GC + D_MODEL])) * pc
    xo_ref[...] = x + _dot(_bf(merged), w_o_ref[...])


def _sample_mixer(xs, pool_st, rconv_st, h0, w, w_bf16, l):
    n = xs.shape[0]
    f32 = jnp.float32
    acts = [xs, pool_st, rconv_st, h0]
    small = [w[k] for k in _MIXER_SMALL + ('ws0', 'bs0')]
    big = [w_bf16[k] for k in _MIXER_BIG]
    out_shape = [jax.ShapeDtypeStruct((n, D_MODEL), f32),
                 jax.ShapeDtypeStruct((POOL_STATE, n, D_POOL), f32),
                 jax.ShapeDtypeStruct((CONV_WIDTH - 1, n, D_RNN), f32),
                 jax.ShapeDtypeStruct((n, D_RNN), f32),
                 jax.ShapeDtypeStruct((n, D_CHUNK), f32)]
    return pl.pallas_call(
        _sample_mixer_kernel,
        grid=(1,),
        in_specs=[_whole_spec(xs)] + [_layer_spec(a, l) for a in acts[1:] + small]
                 + [_whole_spec(a) for a in big],
        out_specs=[pl.BlockSpec(s.shape, lambda t, nd=len(s.shape): (0,) * nd) for s in out_shape],
        out_shape=out_shape,
        compiler_params=pltpu.CompilerParams(dimension_semantics=("arbitrary",),
                                             vmem_limit_bytes=VMEM_LIMIT),
        name="sample_mixer",
    )(*acts, *small, *big)


def _sample_ffn_kernel(x_ref, st_ref, g2_ref, fcw_ref, fcb_ref, gf_ref, wg_ref, wu_ref, wd_ref,
                       xo_ref, st_o_ref, *, last_layer):
    x = x_ref[...]
    xn = _bf(_rms(x, g2_ref[...]))
    g_pre = _dot(xn, wg_ref[...])
    y = fcb_ref[...] + fcw_ref[FFN_CONV - 1:FFN_CONV, :] * g_pre
    for m in range(FFN_CONV - 1):
        y = y + fcw_ref[m:m + 1, :] * st_ref[m]
    st_o_ref[:FFN_CONV - 2] = st_ref[1:]
    st_o_ref[FFN_CONV - 2] = g_pre
    h = _gelu(y) * _dot(xn, wu_ref[...])
    out = x + _dot(_bf(h), wd_ref[...])
    if last_layer:
        out = _rms(out, gf_ref[...])
    xo_ref[...] = out


def _sample_ffn(xs, st, w, w_bf16, l, last_layer):
    n = xs.shape[0]
    small = [w[k] for k in _FFN_SMALL]
    big = [w_bf16[k] for k in _FFN_BIG]
    out_shape = [jax.ShapeDtypeStruct((n, D_MODEL), jnp.float32),
                 jax.ShapeDtypeStruct((FFN_CONV - 1, n, D_FF), jnp.float32)]
    return pl.pallas_call(
        functools.partial(_sample_ffn_kernel, last_layer=last_layer),
        grid=(1,),
        in_specs=[_whole_spec(xs), _layer_spec(st, l)] + [_layer_spec(a, l) for a in small]
                 + [_whole_spec(w['gf'])] + [_whole_spec(a) for a in big],
        out_specs=[pl.BlockSpec(s.shape, lambda t, nd=len(s.shape): (0,) * nd) for s in out_shape],
        out_shape=out_shape,
        compiler_params=pltpu.CompilerParams(dimension_semantics=("arbitrary",),
                                             vmem_limit_bytes=VMEM_LIMIT),
        name="sample_ffn",
    )(xs, st, *small, w['gf'], *big)


def _prepare_weights(p):
    bf16 = jnp.bfloat16
    depth = p['w_in'].shape[0]
    row = lambda a: a.reshape(depth, 1, -1)
    return {
        'g1': row(p['norm1_g']), 'w_in': p['w_in'],
        'pool_w': p['pool_w'].astype(bf16), 'pool_scale': row(p['pool_scale']),
        'cw': p['rnn_conv_w'], 'cb': row(p['rnn_conv_b']),
        'lru_w': jnp.concatenate([p['lru_wa'], p['lru_wx']], axis=-1).astype(bf16),
        'lru_b': jnp.stack([p['lru_ba'], p['lru_bx']], axis=1),
        'lam': row(p['lru_lambda']), 'vg': row(p['chunk_vnorm_g']),
        'ws': p['chunk_ws'].astype(bf16),
        'bs_tm': jnp.repeat(jnp.swapaxes(p['chunk_bs'], 1, 2), PB, axis=1),
        'ws0': row(jnp.repeat(p['chunk_ws'][:, :, 0, 0], CHUNK_GW, axis=1)),
        'bs0': row(jnp.repeat(p['chunk_bs'][:, :, 0], CHUNK_GW, axis=1)),
        'w_pa': p['w_pa'], 'w_pb': p['w_pb'], 'w_pc': p['w_pc'], 'w_o': p['w_o'],
        'g2': row(p['norm2_g']), 'wg': p['ffn_wg'], 'wu': p['ffn_wu'], 'fcw': p['ffn_conv_w'],
        'fcb': row(p['ffn_conv_b']), 'wd': p['ffn_wd'],
        'gf': p['final_norm_g'].reshape(1, -1),
    }


def _batch_major(rows_tm, n):
    return jnp.transpose(rows_tm.reshape(-1, PB, rows_tm.shape[-1])[-n:], (1, 0, 2))


def kernel(x_prompt, x_sample, state_pool, state_rnn_conv, state_rnn_h, state_ffn_conv, norm1_g, w_in, pool_w, pool_scale, rnn_conv_w, rnn_conv_b, lru_wa, lru_ba, lru_wx, lru_bx, lru_lambda, chunk_vnorm_g, chunk_ws, chunk_bs, w_pa, w_pb, w_pc, w_o, norm2_g, ffn_wg, ffn_wu, ffn_conv_w, ffn_conv_b, ffn_wd, final_norm_g):
    p = dict(norm1_g=norm1_g, w_in=w_in, pool_w=pool_w, pool_scale=pool_scale,
             rnn_conv_w=rnn_conv_w, rnn_conv_b=rnn_conv_b, lru_wa=lru_wa, lru_ba=lru_ba,
             lru_wx=lru_wx, lru_bx=lru_bx, lru_lambda=lru_lambda, chunk_vnorm_g=chunk_vnorm_g,
             chunk_ws=chunk_ws, chunk_bs=chunk_bs, w_pa=w_pa, w_pb=w_pb, w_pc=w_pc, w_o=w_o,
             norm2_g=norm2_g, ffn_wg=ffn_wg, ffn_wu=ffn_wu, ffn_conv_w=ffn_conv_w,
             ffn_conv_b=ffn_conv_b, ffn_wd=ffn_wd, final_norm_g=final_norm_g)
    depth = w_in.shape[0]
    bp, tp, _ = x_prompt.shape
    ns = x_sample.shape[0]
    assert bp == PB and tp % TILE_T == 0 and x_sample.shape[1] == 1
    w = _prepare_weights(p)

    xp = x_prompt
    xs = x_sample.reshape(ns, D_MODEL)
    pool_st = jnp.swapaxes(state_pool, 1, 2)
    rconv_st = jnp.swapaxes(state_rnn_conv, 1, 2)
    ffn_st = jnp.swapaxes(state_ffn_conv, 1, 2)
    outs = {k: [] for k in ('pool_p', 'pool_s', 'rc_p', 'rc_s', 'h_p', 'h_s', 'ff_p', 'ff_s', 'cv_s')}
    for l in range(depth):
        last = l == depth - 1
        (xp, pool_tm, rconv_tm, h_p), mixer_bf16 = _prompt_mixer(xp, w, l, batch_major_in=(l == 0))
        (xp, ffn_tm), ffn_bf16 = _prompt_ffn(xp, w, l, last)
        xs, pool_s, rconv_s, h_s, v_s = _sample_mixer(xs, pool_st, rconv_st, state_rnn_h, w,
                                                      mixer_bf16, l)
        xs, ffn_s = _sample_ffn(xs, ffn_st, w, ffn_bf16, l, last)
        outs['pool_p'].append(_batch_major(pool_tm, POOL_STATE))
        outs['rc_p'].append(_batch_major(rconv_tm, CONV_WIDTH - 1))
        outs['h_p'].append(h_p)
        outs['ff_p'].append(_batch_major(ffn_tm, FFN_CONV - 1))
        outs['pool_s'].append(jnp.swapaxes(pool_s, 0, 1))
        outs['rc_s'].append(jnp.swapaxes(rconv_s, 0, 1))
        outs['h_s'].append(h_s)
        outs['ff_s'].append(jnp.swapaxes(ffn_s, 0, 1))
        outs['cv_s'].append(v_s.reshape(ns, 1, D_CHUNK))
    st = jnp.stack
    return (xp, xs.reshape(ns, 1, D_MODEL), st(outs['pool_p']), st(outs['pool_s']),
            st(outs['rc_p']), st(outs['rc_s']), st(outs['h_p']), st(outs['h_s']),
            st(outs['ff_p']), st(outs['ff_s']), st(outs['cv_s']))
```

```python
import functools
import math

import jax
import jax.numpy as jnp
from jax import lax
from jax.experimental import pallas as pl
from jax.experimental.pallas import tpu as pltpu

D_MODEL = 1024
D_POOL = 512
POOL_WINDOWS = (2, 4, 8, 16)
POOL_GW = 128
POOL_STATE = 15
D_RNN = 1024
RNN_HEADS = 8
RNN_HD = 128
CONV_WIDTH = 4
LRU_C = 8.0
D_CHUNK = 512
CHUNK = 128
CHUNK_GROUPS = 4
CHUNK_GW = 128
D_FF = 3072
FFN_CONV = 3
EPS = 1e-6
PAST_LEN = 16384

C_POOL = 0
C_BX = D_POOL
C_BG = C_BX + D_RNN
C_CU = C_BG + D_RNN
C_CV = C_CU + D_CHUNK
C_GA = C_CV + D_CHUNK
C_GB = C_GA + D_MODEL
C_GC = C_GB + D_MODEL

PB = 8
PB_SHIFT = PB.bit_length() - 1
TILE_T = CHUNK
TILE_R = TILE_T * PB
SUB_T = 64
SUB_R = SUB_T * PB
FFN_R = 512
FFN_T = FFN_R // PB
FFN_C = 1536
VMEM_LIMIT = 60 * 1024 * 1024

_GELU_K = math.sqrt(2.0 / math.pi)


def _dot(a, b):
    return jnp.dot(a, b, preferred_element_type=jnp.float32)


def _bf(x):
    return x.astype(jnp.bfloat16)


def _gelu(x):
    return 0.5 * x * (1.0 + jnp.tanh(_GELU_K * (x + 0.044715 * (x * x * x))))


def _sigmoid(x):
    return 0.5 * (jnp.tanh(0.5 * x) + 1.0)


def _rms(x, g):
    return x * lax.rsqrt(jnp.mean(x * x, axis=-1, keepdims=True) + EPS) * g


def _lru_log_decay_scale(lam):
    z = -lam
    sp = jnp.maximum(z, 0.0) + jnp.log1p(jnp.exp(-jnp.abs(z)))
    return -LRU_C * sp


def _lru_coeffs(bc, lru_w_ref, lru_b_ref, dscale, after_head=None):
    a_parts, b_parts = [], []
    for h in range(RNN_HEADS):
        if after_head is not None and h > 0:
            after_head(h - 1)
        cs = slice(h * RNN_HD, (h + 1) * RNN_HD)
        bch = bc[:, cs]
        ri = _dot(_bf(bch), lru_w_ref[h])
        r = _sigmoid(ri[:, :RNN_HD] + lru_b_ref[0:1, cs])
        i = _sigmoid(ri[:, RNN_HD:] + lru_b_ref[1:2, cs])
        log_a = dscale[:, cs] * r
        a = jnp.exp(log_a)
        a_parts.append(a)
        b_parts.append(jnp.sqrt((1.0 - a) * (1.0 + a)) * (i * bch))
    return jnp.concatenate(a_parts, axis=1), jnp.concatenate(b_parts, axis=1)


def _pool_project(d_parts, pool_w_ref, pool_scale_ref, w_pa_ref):
    ya = [_dot(_bf(d), pool_w_ref[g]) * pool_scale_ref[:, g * POOL_GW:(g + 1) * POOL_GW]
          for g, d in enumerate(d_parts)]
    return _dot(_bf(jnp.concatenate(ya, axis=1)), w_pa_ref[...])


WEIGHT_STAGE_SLOTS = 3


def _load_weight_as_bf16(src_hbm, layer, dst_ref, rows_per_chunk):
    n_rows, n_cols = dst_ref.shape
    assert n_rows % rows_per_chunk == 0
    n_chunks = n_rows // rows_per_chunk
    n_slots = WEIGHT_STAGE_SLOTS

    def body(stage, sem):
        def copy(i):
            rows = pl.ds(i * rows_per_chunk, rows_per_chunk)
            return pltpu.make_async_copy(src_hbm.at[layer, rows, :], stage.at[i % n_slots],
                                         sem.at[i % n_slots])

        for i in range(min(n_slots - 1, n_chunks)):
            copy(i).start()
        for i in range(n_chunks):
            if i + n_slots - 1 < n_chunks:
                copy(i + n_slots - 1).start()
            copy(i).wait()
            dst_ref[i * rows_per_chunk:(i + 1) * rows_per_chunk, :] = _bf(stage[i % n_slots])

    pl.run_scoped(body, pltpu.VMEM((n_slots, rows_per_chunk, n_cols), jnp.float32),
                  pltpu.SemaphoreType.DMA((n_slots,)))


def _prompt_tile_in_copies(x_hbm, xbuf, sem, tile, slot):
    return [pltpu.make_async_copy(x_hbm.at[s, pl.ds(tile * TILE_T, TILE_T), :],
                                  xbuf.at[slot, :, s, :], sem.at[slot, s]) for s in range(PB)]


def _prompt_tile_out_copies(ybuf, y_hbm, sem, tile, slot):
    return [pltpu.make_async_copy(ybuf.at[slot, :, s, :],
                                  y_hbm.at[s, pl.ds(tile * FFN_T, FFN_T), :], sem.at[slot, s])
            for s in range(PB)]


def _publish_copies(pairs, sem):
    return [pltpu.make_async_copy(src, dst, sem.at[i]) for i, (src, dst) in enumerate(pairs)]


def _build_time_major_spatial_weights(ws_ref, kron_refs):
    row_t = lax.shift_right_logical(lax.broadcasted_iota(jnp.int32, (SUB_R, SUB_T), 0), PB_SHIFT)
    repeat_rows = _bf(row_t == lax.broadcasted_iota(jnp.int32, (SUB_R, SUB_T), 1))
    for k, kron_ref in enumerate(kron_refs):
        n_t, n_r = (k + 1) * SUB_T, (k + 1) * SUB_R
        col_t = lax.shift_right_logical(lax.broadcasted_iota(jnp.int32, (n_t, n_r), 1), PB_SHIFT)
        repeat_cols = _bf(col_t == lax.broadcasted_iota(jnp.int32, (n_t, n_r), 0))
        rr = lax.broadcasted_iota(jnp.int32, (SUB_R, n_r), 0)
        cc = lax.broadcasted_iota(jnp.int32, (SUB_R, n_r), 1)
        same_seq = (rr & (PB - 1)) == (cc & (PB - 1))
        causal = (lax.shift_right_logical(rr, PB_SHIFT) + k * SUB_T) >= lax.shift_right_logical(cc, PB_SHIFT)
        keep = same_seq & causal
        for g in range(CHUNK_GROUPS):
            rows = _dot(repeat_rows, ws_ref[g, k * SUB_T:(k + 1) * SUB_T, 0:n_t])
            full = _dot(_bf(rows), repeat_cols)
            kron_ref[g] = _bf(jnp.where(keep, full, 0.0))


def _prompt_mixer_kernel(x_ref, g1_ref, pool_w_ref, pool_scale_ref, cw_ref, cb_ref,
                         lru_w_ref, lru_b_ref, lam_ref, vg_ref, ws_ref, bs_ref,
                         w_in_ref, w_pa_ref, w_pb_ref, w_pc_ref, w_o_ref,
                         xo_ref, pool_o_ref, rconv_o_ref, h_o_ref,
                         pool_prev, conv_prev, h_carry, *scratch, batch_major_in):
    t = pl.program_id(0)
    n_sub = TILE_T // SUB_T
    kron_refs = scratch[:n_sub]

    @pl.when(t == 0)
    def _():
        pool_prev[...] = jnp.zeros_like(pool_prev)
        conv_prev[...] = jnp.zeros_like(conv_prev)
        h_carry[...] = jnp.zeros_like(h_carry)
        _build_time_major_spatial_weights(ws_ref, kron_refs)

    if batch_major_in:
        xbuf, x_sem = scratch[n_sub:]
        slot = t % 2

        @pl.when(t == 0)
        def _():
            for c in _prompt_tile_in_copies(x_ref, xbuf, x_sem, 0, 0):
                c.start()

        @pl.when(t + 1 < pl.num_programs(0))
        def _():
            for c in _prompt_tile_in_copies(x_ref, xbuf, x_sem, t + 1, 1 - slot):
                c.start()

        for c in _prompt_tile_in_copies(x_ref, xbuf, x_sem, t, slot):
            c.wait()

    dscale = _lru_log_decay_scale(lam_ref[...])
    g1 = g1_ref[...]
    pool_pre = pool_prev[...]
    conv_pre = conv_prev[...]
    h = h_carry[...]
    npre = pool_pre.shape[0]
    v_parts = []

    def load_normed(k):
        if batch_major_in:
            x = xbuf[slot, k * SUB_T:(k + 1) * SUB_T].reshape(SUB_R, D_MODEL)
        else:
            x = x_ref[k * SUB_R:(k + 1) * SUB_R, :]
        return x, _bf(_rms(x, g1))

    x, xn = load_normed(0)
    a_in = _dot(xn, w_in_ref[:, C_POOL:C_POOL + D_POOL])
    b_x = _dot(xn, w_in_ref[:, C_BX:C_BX + D_RNN])
    for k in range(n_sub):
        r0 = k * SUB_R
        proj = lambda c0, n, xn=xn: _dot(xn, w_in_ref[:, c0:c0 + n])

        a_ext = jnp.concatenate([pool_pre, a_in], axis=0)
        pos = t * TILE_T + k * SUB_T + lax.shift_right_logical(
            lax.broadcasted_iota(jnp.int32, (SUB_R, POOL_GW), 0), PB_SHIFT)
        d_parts = []
        for g, w in enumerate(POOL_WINDOWS):
            cs = slice(g * POOL_GW, (g + 1) * POOL_GW)
            s = a_ext[npre - (w - 1) * PB:, cs]
            sh = 1
            while sh < w:
                s = s[sh * PB:] + s[:-sh * PB]
                sh *= 2
            cnt = jnp.minimum(pos + 1, w).astype(jnp.float32)
            d_parts.append(s / cnt - a_in[:, cs])
        pool_pre = a_ext[SUB_R:]
        pa = _pool_project(d_parts, pool_w_ref, pool_scale_ref, w_pa_ref)

        b_ext = jnp.concatenate([conv_pre, b_x], axis=0)
        bc = cb_ref[...] + cw_ref[CONV_WIDTH - 1:CONV_WIDTH, :] * b_x
        for m in range(1, CONV_WIDTH):
            lo = (CONV_WIDTH - 1 - m) * PB
            bc = bc + cw_ref[CONV_WIDTH - 1 - m:CONV_WIDTH - m, :] * b_ext[lo:lo + SUB_R]
        conv_pre = b_ext[SUB_R:]
        late = {}
        late_cols = {0: ('bg', C_BG, D_RNN), 1: ('cu', C_CU, D_CHUNK), 2: ('cv', C_CV, D_CHUNK),
                     3: ('ga', C_GA, D_MODEL), 4: ('gb', C_GB, D_MODEL), 5: ('gc', C_GC, D_MODEL)}

        def issue_late(hd):
            if hd in late_cols:
                name, c0, n = late_cols[hd]
                late[name] = proj(c0, n)

        a_dec, b_inp = _lru_coeffs(bc, lru_w_ref, lru_b_ref, dscale, after_head=issue_late)
        hs = []
        for j in range(SUB_T):
            h = a_dec[j * PB:(j + 1) * PB] * h + b_inp[j * PB:(j + 1) * PB]
            hs.append(h)
        pb = _dot(_bf(_gelu(late['bg']) * jnp.concatenate(hs, axis=0)), w_pb_ref[...])

        u = _gelu(late['cu'])
        v = _rms(_gelu(late['cv']), vg_ref[...])
        v_parts.append(_bf(v))
        v_seen = jnp.concatenate(v_parts, axis=0)
        mix = []
        for g in range(CHUNK_GROUPS):
            cs = slice(g * CHUNK_GW, (g + 1) * CHUNK_GW)
            m = _dot(kron_refs[k][g], v_seen[:, cs])
            mix.append(m + bs_ref[r0:r0 + SUB_R, g:g + 1])
        pc = _dot(_bf(u * jnp.concatenate(mix, axis=1)), w_pc_ref[...])

        more = k + 1 < n_sub
        if more:
            x_nxt, xn_nxt = load_normed(k + 1)
        merged = _sigmoid(late['ga']) * pa
        if more:
            a_nxt = _dot(xn_nxt, w_in_ref[:, C_POOL:C_POOL + D_POOL])
        merged = merged + _sigmoid(late['gb']) * pb
        if more:
            b_nxt = _dot(xn_nxt, w_in_ref[:, C_BX:C_BX + D_RNN])
        merged = merged + _sigmoid(late['gc']) * pc
        xo_ref[r0:r0 + SUB_R, :] = x + _dot(_bf(merged), w_o_ref[...])
        if more:
            x, xn, a_in, b_x = x_nxt, xn_nxt, a_nxt, b_nxt

    pool_prev[...] = pool_pre
    conv_prev[...] = conv_pre
    h_carry[...] = h
    pool_o_ref[...] = pool_pre
    rconv_o_ref[...] = conv_pre
    h_o_ref[...] = h


def _layer_spec(a, l):
    nd = a.ndim
    return pl.BlockSpec((None,) + a.shape[1:], lambda *_: (l,) + (0,) * (nd - 1),
                        pipeline_mode=pl.Buffered(1))


def _whole_spec(a):
    nd = a.ndim
    return pl.BlockSpec(a.shape, lambda *_: (0,) * nd, pipeline_mode=pl.Buffered(1))


_MIXER_SMALL = ('g1', 'pool_w', 'pool_scale', 'cw', 'cb', 'lru_w', 'lru_b', 'lam', 'vg')
_MIXER_BIG = ('w_in', 'w_pa', 'w_pb', 'w_pc', 'w_o')
_FFN_SMALL = ('g2', 'fcw', 'fcb')
_FFN_BIG = ('wg', 'wu', 'wd')
_ANY = pl.BlockSpec(memory_space=pl.ANY)


def _prompt_mixer(x, w, l, batch_major_in):
    f32, bf16 = jnp.float32, jnp.bfloat16
    scratch_in = []
    if batch_major_in:
        n_rows = x.shape[0] * x.shape[1]
        x_spec = _ANY
        scratch_in = [pltpu.VMEM((2, TILE_T, PB, D_MODEL), f32), pltpu.SemaphoreType.DMA((2, PB))]
    else:
        n_rows = x.shape[0]
        x_spec = pl.BlockSpec((TILE_R, D_MODEL), lambda t: (t, 0))
    weights = [w[k] for k in _MIXER_SMALL + ('ws', 'bs_tm')] + [w[k + '_bf'] for k in _MIXER_BIG]
    pool_rows = (POOL_STATE + 1) * PB
    conv_rows = (CONV_WIDTH - 1) * PB
    return pl.pallas_call(
        functools.partial(_prompt_mixer_kernel, batch_major_in=batch_major_in),
        grid=(n_rows // TILE_R,),
        in_specs=[x_spec] + [_layer_spec(a, l) for a in weights],
        out_specs=[pl.BlockSpec((TILE_R, D_MODEL), lambda t: (t, 0)),
                   pl.BlockSpec((pool_rows, D_POOL), lambda t: (0, 0)),
                   pl.BlockSpec((conv_rows, D_RNN), lambda t: (0, 0)),
                   pl.BlockSpec((PB, D_RNN), lambda t: (0, 0))],
        out_shape=[jax.ShapeDtypeStruct((n_rows, D_MODEL), f32),
                   jax.ShapeDtypeStruct((pool_rows, D_POOL), f32),
                   jax.ShapeDtypeStruct((conv_rows, D_RNN), f32),
                   jax.ShapeDtypeStruct((PB, D_RNN), f32)],
        scratch_shapes=[pltpu.VMEM((pool_rows, D_POOL), f32),
                        pltpu.VMEM((conv_rows, D_RNN), f32),
                        pltpu.VMEM((PB, D_RNN), f32)]
                       + [pltpu.VMEM((CHUNK_GROUPS, SUB_R, (k + 1) * SUB_R), bf16)
                          for k in range(TILE_T // SUB_T)]
                       + scratch_in,
        compiler_params=pltpu.CompilerParams(dimension_semantics=("arbitrary",),
                                             vmem_limit_bytes=VMEM_LIMIT),
        name="prompt_mixer",
    )(x, *weights)


def _prompt_ffn_kernel(x_ref, g2_ref, fcw_ref, fcb_ref, gf_ref, wg_hbm, wu_hbm, wd_hbm,
                       xo_ref, ffn_o_ref, wg_pub, wu_pub, wd_pub,
                       g_prev, wg_ref, wu_ref, wd_ref, pub_sem, *out_scratch, layer, last_layer):
    t = pl.program_id(0)
    n_steps = pl.num_programs(0)
    publish = _publish_copies([(wg_ref, wg_pub), (wu_ref, wu_pub), (wd_ref, wd_pub)], pub_sem)

    @pl.when(t == 0)
    def _():
        g_prev[...] = jnp.zeros_like(g_prev)
        _load_weight_as_bf16(wg_hbm, layer, wg_ref, 256)
        _load_weight_as_bf16(wu_hbm, layer, wu_ref, 256)
        _load_weight_as_bf16(wd_hbm, layer, wd_ref, 768)
        for c in publish:
            c.start()

    @pl.when(t == pl.num_programs(0) - 1)
    def _():
        for c in publish:
            c.wait()

    x = x_ref[...]
    xn = _bf(_rms(x, g2_ref[...]))
    acc = x
    npre = (FFN_CONV - 1) * PB
    n_chunks = D_FF // FFN_C
    col = lambda c: slice(c * FFN_C, (c + 1) * FFN_C)
    up_proj = lambda c: (_dot(xn, wg_ref[:, col(c)]), _dot(xn, wu_ref[:, col(c)]))
    nxt = up_proj(0)
    for c in range(n_chunks):
        cs = col(c)
        g_pre, up = nxt
        if c + 1 < n_chunks:
            nxt = up_proj(c + 1)
        g_ext = jnp.concatenate([g_prev[:, cs], g_pre], axis=0)
        y = fcb_ref[:, cs] + fcw_ref[FFN_CONV - 1:FFN_CONV, cs] * g_pre
        for m in range(1, FFN_CONV):
            lo = npre - m * PB
            y = y + fcw_ref[FFN_CONV - 1 - m:FFN_CONV - m, cs] * g_ext[lo:lo + FFN_R]
        g_prev[:, cs] = g_ext[FFN_R:]
        h = _gelu(y) * up
        acc = acc + _dot(_bf(h), wd_ref[cs, :])
    if last_layer:
        ybuf, y_sem = out_scratch
        slot = t % 2
        out_copies = functools.partial(_prompt_tile_out_copies, ybuf, xo_ref, y_sem)

        @pl.when(t >= 2)
        def _():
            for c in out_copies(t - 2, slot):
                c.wait()

        ybuf[slot] = _rms(acc, gf_ref[...]).reshape(FFN_T, PB, D_MODEL)
        for c in out_copies(t, slot):
            c.start()

        @pl.when(t == n_steps - 1)
        def _():
            @pl.when(t >= 1)
            def _():
                for c in out_copies(t - 1, 1 - slot):
                    c.wait()
            for c in out_copies(t, slot):
                c.wait()
    else:
        xo_ref[...] = acc
    ffn_o_ref[...] = g_prev[...]


def _prompt_ffn(x_tm, w, l, last_layer):
    n_rows = x_tm.shape[0]
    row_spec = pl.BlockSpec((FFN_R, D_MODEL), lambda t: (t, 0))
    pre_rows = (FFN_CONV - 1) * PB
    out_scratch = []
    if last_layer:
        out_spec = _ANY
        out_shape = jax.ShapeDtypeStruct((PB, n_rows // PB, D_MODEL), jnp.float32)
        out_scratch = [pltpu.VMEM((2, FFN_T, PB, D_MODEL), jnp.float32),
                       pltpu.SemaphoreType.DMA((2, PB))]
    else:
        out_spec, out_shape = row_spec, jax.ShapeDtypeStruct((n_rows, D_MODEL), jnp.float32)
    small = [w[k] for k in _FFN_SMALL]
    big = [w[k] for k in _FFN_BIG]
    bf16 = jnp.bfloat16
    outs = pl.pallas_call(
        functools.partial(_prompt_ffn_kernel, layer=l, last_layer=last_layer),
        grid=(n_rows // FFN_R,),
        in_specs=[row_spec] + [_layer_spec(a, l) for a in small] + [_whole_spec(w['gf'])]
                 + [_ANY] * len(big),
        out_specs=[out_spec, pl.BlockSpec((pre_rows, D_FF), lambda t: (0, 0))] + [_ANY] * len(big),
        out_shape=[out_shape, jax.ShapeDtypeStruct((pre_rows, D_FF), jnp.float32)]
                  + [jax.ShapeDtypeStruct(a.shape[1:], bf16) for a in big],
        scratch_shapes=[pltpu.VMEM((pre_rows, D_FF), jnp.float32)]
                       + [pltpu.VMEM(a.shape[1:], bf16) for a in big]
                       + [pltpu.SemaphoreType.DMA((len(big),))] + out_scratch,
        compiler_params=pltpu.CompilerParams(dimension_semantics=("arbitrary",),
                                             vmem_limit_bytes=VMEM_LIMIT),
        name="prompt_ffn",
    )(x_tm, *small, w['gf'], *big)
    return outs[:2], dict(zip(_FFN_BIG, outs[2:]))


def _sample_mixer_kernel(x_ref, pool_st_ref, rconv_st_ref, h0_ref,
                         g1_ref, pool_w_ref, pool_scale_ref, cw_ref, cb_ref,
                         lru_w_ref, lru_b_ref, lam_ref, vg_ref, ws0_ref, bs0_ref,
                         w_in_ref, w_pa_ref, w_pb_ref, w_pc_ref, w_o_ref,
                         xo_ref, pool_o_ref, rconv_o_ref, h_o_ref, v_o_ref):
    x = x_ref[...]
    xn = _bf(_rms(x, g1_ref[...]))

    a_in = _dot(xn, w_in_ref[:, C_POOL:C_POOL + D_POOL])
    d_parts = []
    for g, w in enumerate(POOL_WINDOWS):
        s = a_in[:, g * POOL_GW:(g + 1) * POOL_GW]
        for m in range(1, w):
            s = s + pool_st_ref[POOL_STATE - m, :, g * POOL_GW:(g + 1) * POOL_GW]
        cnt = float(min(PAST_LEN + 1, w))
        d_parts.append(s / cnt - a_in[:, g * POOL_GW:(g + 1) * POOL_GW])
    pool_o_ref[:POOL_STATE - 1] = pool_st_ref[1:]
    pool_o_ref[POOL_STATE - 1] = a_in
    pa = _pool_project(d_parts, pool_w_ref, pool_scale_ref, w_pa_ref)

    b_x = _dot(xn, w_in_ref[:, C_BX:C_BX + D_RNN])
    bc = cb_ref[...] + cw_ref[CONV_WIDTH - 1:CONV_WIDTH, :] * b_x
    for m in range(CONV_WIDTH - 1):
        bc = bc + cw_ref[m:m + 1, :] * rconv_st_ref[m]
    rconv_o_ref[:CONV_WIDTH - 2] = rconv_st_ref[1:]
    rconv_o_ref[CONV_WIDTH - 2] = b_x
    a_dec, b_inp = _lru_coeffs(bc, lru_w_ref, lru_b_ref, _lru_log_decay_scale(lam_ref[...]))
    h = a_dec * h0_ref[...] + b_inp
    h_o_ref[...] = h
    b_gate = _dot(xn, w_in_ref[:, C_BG:C_BG + D_RNN])
    pb = _dot(_bf(_gelu(b_gate) * h), w_pb_ref[...])

    u = _gelu(_dot(xn, w_in_ref[:, C_CU:C_CU + D_CHUNK]))
    v = _rms(_gelu(_dot(xn, w_in_ref[:, C_CV:C_CV + D_CHUNK])), vg_ref[...])
    v_o_ref[...] = v
    mix = ws0_ref[...] * v + bs0_ref[...]
    pc = _dot(_bf(u * mix), w_pc_ref[...])

    merged = _sigmoid(_dot(xn, w_in_ref[:, C_GA:C_GA + D_MODEL])) * pa
    merged = merged + _sigmoid(_dot(xn, w_in_ref[:, C_GB:C_GB + D_MODEL])) * pb
    merged = merged + _sigmoid(_dot(xn, w_in_ref[:, C_GC:C_GC + D_MODEL])) * pc
    xo_ref[...] = x + _dot(_bf(merged), w_o_ref[...])


def _sample_mixer(xs, pool_st, rconv_st, h0, w, l):
    n = xs.shape[0]
    f32 = jnp.float32
    acts = [xs, pool_st, rconv_st, h0]
    weights = [w[k] for k in _MIXER_SMALL + ('ws0', 'bs0')] + [w[k + '_bf'] for k in _MIXER_BIG]
    out_shape = [jax.ShapeDtypeStruct((n, D_MODEL), f32),
                 jax.ShapeDtypeStruct((POOL_STATE, n, D_POOL), f32),
                 jax.ShapeDtypeStruct((CONV_WIDTH - 1, n, D_RNN), f32),
                 jax.ShapeDtypeStruct((n, D_RNN), f32),
                 jax.ShapeDtypeStruct((n, D_CHUNK), f32)]
    return pl.pallas_call(
        _sample_mixer_kernel,
        grid=(1,),
        in_specs=[_whole_spec(xs)] + [_layer_spec(a, l) for a in acts[1:] + weights],
        out_specs=[pl.BlockSpec(s.shape, lambda t, nd=len(s.shape): (0,) * nd) for s in out_shape],
        out_shape=out_shape,
        compiler_params=pltpu.CompilerParams(dimension_semantics=("arbitrary",),
                                             vmem_limit_bytes=VMEM_LIMIT),
        name="sample_mixer",
    )(*acts, *weights)


def _sample_ffn_kernel(x_ref, st_ref, g2_ref, fcw_ref, fcb_ref, gf_ref, wg_ref, wu_ref, wd_ref,
                       xo_ref, st_o_ref, *, last_layer):
    x = x_ref[...]
    xn = _bf(_rms(x, g2_ref[...]))
    g_pre = _dot(xn, wg_ref[...])
    y = fcb_ref[...] + fcw_ref[FFN_CONV - 1:FFN_CONV, :] * g_pre
    for m in range(FFN_CONV - 1):
        y = y + fcw_ref[m:m + 1, :] * st_ref[m]
    st_o_ref[:FFN_CONV - 2] = st_ref[1:]
    st_o_ref[FFN_CONV - 2] = g_pre
    h = _gelu(y) * _dot(xn, wu_ref[...])
    out = x + _dot(_bf(h), wd_ref[...])
    if last_layer:
        out = _rms(out, gf_ref[...])
    xo_ref[...] = out


def _sample_ffn(xs, st, w, w_bf16, l, last_layer):
    n = xs.shape[0]
    small = [w[k] for k in _FFN_SMALL]
    big = [w_bf16[k] for k in _FFN_BIG]
    out_shape = [jax.ShapeDtypeStruct((n, D_MODEL), jnp.float32),
                 jax.ShapeDtypeStruct((FFN_CONV - 1, n, D_FF), jnp.float32)]
    return pl.pallas_call(
        functools.partial(_sample_ffn_kernel, last_layer=last_layer),
        grid=(1,),
        in_specs=[_whole_spec(xs), _layer_spec(st, l)] + [_layer_spec(a, l) for a in small]
                 + [_whole_spec(w['gf'])] + [_whole_spec(a) for a in big],
        out_specs=[pl.BlockSpec(s.shape, lambda t, nd=len(s.shape): (0,) * nd) for s in out_shape],
        out_shape=out_shape,
        compiler_params=pltpu.CompilerParams(dimension_semantics=("arbitrary",),
                                             vmem_limit_bytes=VMEM_LIMIT),
        name="sample_ffn",
    )(xs, st, *small, w['gf'], *big)


def _prepare_weights(p):
    bf16 = jnp.bfloat16
    depth = p['w_in'].shape[0]
    row = lambda a: a.reshape(depth, 1, -1)
    return {
        'g1': row(p['norm1_g']),
        'pool_w': p['pool_w'].astype(bf16), 'pool_scale': row(p['pool_scale']),
        'cw': p['rnn_conv_w'], 'cb': row(p['rnn_conv_b']),
        'lru_w': jnp.concatenate([p['lru_wa'], p['lru_wx']], axis=-1).astype(bf16),
        'lru_b': jnp.stack([p['lru_ba'], p['lru_bx']], axis=1),
        'lam': row(p['lru_lambda']), 'vg': row(p['chunk_vnorm_g']),
        'ws': p['chunk_ws'].astype(bf16),
        'bs_tm': jnp.repeat(jnp.swapaxes(p['chunk_bs'], 1, 2), PB, axis=1),
        'ws0': row(jnp.repeat(p['chunk_ws'][:, :, 0, 0], CHUNK_GW, axis=1)),
        'bs0': row(jnp.repeat(p['chunk_bs'][:, :, 0], CHUNK_GW, axis=1)),
        'w_in_bf': p['w_in'].astype(bf16), 'w_pa_bf': p['w_pa'].astype(bf16),
        'w_pb_bf': p['w_pb'].astype(bf16), 'w_pc_bf': p['w_pc'].astype(bf16),
        'w_o_bf': p['w_o'].astype(bf16),
        'g2': row(p['norm2_g']), 'wg': p['ffn_wg'], 'wu': p['ffn_wu'], 'fcw': p['ffn_conv_w'],
        'fcb': row(p['ffn_conv_b']), 'wd': p['ffn_wd'],
        'gf': p['final_norm_g'].reshape(1, -1),
    }


def _batch_major(rows_tm, n):
    return jnp.transpose(rows_tm.reshape(-1, PB, rows_tm.shape[-1])[-n:], (1, 0, 2))


def kernel(x_prompt, x_sample, state_pool, state_rnn_conv, state_rnn_h, state_ffn_conv, norm1_g, w_in, pool_w, pool_scale, rnn_conv_w, rnn_conv_b, lru_wa, lru_ba, lru_wx, lru_bx, lru_lambda, chunk_vnorm_g, chunk_ws, chunk_bs, w_pa, w_pb, w_pc, w_o, norm2_g, ffn_wg, ffn_wu, ffn_conv_w, ffn_conv_b, ffn_wd, final_norm_g):
    p = dict(norm1_g=norm1_g, w_in=w_in, pool_w=pool_w, pool_scale=pool_scale,
             rnn_conv_w=rnn_conv_w, rnn_conv_b=rnn_conv_b, lru_wa=lru_wa, lru_ba=lru_ba,
             lru_wx=lru_wx, lru_bx=lru_bx, lru_lambda=lru_lambda, chunk_vnorm_g=chunk_vnorm_g,
             chunk_ws=chunk_ws, chunk_bs=chunk_bs, w_pa=w_pa, w_pb=w_pb, w_pc=w_pc, w_o=w_o,
             norm2_g=norm2_g, ffn_wg=ffn_wg, ffn_wu=ffn_wu, ffn_conv_w=ffn_conv_w,
             ffn_conv_b=ffn_conv_b, ffn_wd=ffn_wd, final_norm_g=final_norm_g)
    depth = w_in.shape[0]
    bp, tp, _ = x_prompt.shape
    ns = x_sample.shape[0]
    assert bp == PB and tp % TILE_T == 0 and x_sample.shape[1] == 1
    w = _prepare_weights(p)

    xp = x_prompt
    xs = x_sample.reshape(ns, D_MODEL)
    pool_st = jnp.swapaxes(state_pool, 1, 2)
    rconv_st = jnp.swapaxes(state_rnn_conv, 1, 2)
    ffn_st = jnp.swapaxes(state_ffn_conv, 1, 2)
    outs = {k: [] for k in ('pool_p', 'pool_s', 'rc_p', 'rc_s', 'h_p', 'h_s', 'ff_p', 'ff_s', 'cv_s')}
    for l in range(depth):
        last = l == depth - 1
        xp, pool_tm, rconv_tm, h_p = _prompt_mixer(xp, w, l, batch_major_in=(l == 0))
        (xp, ffn_tm), ffn_bf16 = _prompt_ffn(xp, w, l, last)
        xs, pool_s, rconv_s, h_s, v_s = _sample_mixer(xs, pool_st, rconv_st, state_rnn_h, w, l)
        xs, ffn_s = _sample_ffn(xs, ffn_st, w, ffn_bf16, l, last)
        outs['pool_p'].append(_batch_major(pool_tm, POOL_STATE))
        outs['rc_p'].append(_batch_major(rconv_tm, CONV_WIDTH - 1))
        outs['h_p'].append(h_p)
        outs['ff_p'].append(_batch_major(ffn_tm, FFN_CONV - 1))
        outs['pool_s'].append(jnp.swapaxes(pool_s, 0, 1))
        outs['rc_s'].append(jnp.swapaxes(rconv_s, 0, 1))
        outs['h_s'].append(h_s)
        outs['ff_s'].append(jnp.swapaxes(ffn_s, 0, 1))
        outs['cv_s'].append(v_s.reshape(ns, 1, D_CHUNK))
    st = jnp.stack
    return (xp, xs.reshape(ns, 1, D_MODEL), st(outs['pool_p']), st(outs['pool_s']),
            st(outs['rc_p']), st(outs['rc_s']), st(outs['h_p']), st(outs['h_s']),
            st(outs['ff_p']), st(outs['ff_s']), st(outs['cv_s']))
```

```python
import functools
import math

import jax
import jax.numpy as jnp
from jax import lax
from jax.experimental import pallas as pl
from jax.experimental.pallas import tpu as pltpu

D_MODEL = 1024
D_POOL = 512
POOL_WINDOWS = (2, 4, 8, 16)
POOL_GW = 128
POOL_STATE = 15
D_RNN = 1024
RNN_HEADS = 8
RNN_HD = 128
CONV_WIDTH = 4
LRU_C = 8.0
D_CHUNK = 512
CHUNK = 128
CHUNK_GROUPS = 4
CHUNK_GW = 128
D_FF = 3072
FFN_CONV = 3
EPS = 1e-6
PAST_LEN = 16384

C_POOL = 0
C_BX = D_POOL
C_BG = C_BX + D_RNN
C_CU = C_BG + D_RNN
C_CV = C_CU + D_CHUNK
C_GA = C_CV + D_CHUNK
C_GB = C_GA + D_MODEL
C_GC = C_GB + D_MODEL

PB = 8
PB_SHIFT = PB.bit_length() - 1
TILE_T = CHUNK
TILE_R = TILE_T * PB
SUB_T = 64
SUB_R = SUB_T * PB
FFN_R = 512
FFN_T = FFN_R // PB
FFN_C = 1536
VMEM_LIMIT = 60 * 1024 * 1024

_GELU_K = math.sqrt(2.0 / math.pi)


def _dot(a, b):
    return jnp.dot(a, b, preferred_element_type=jnp.float32)


def _bf(x):
    return x.astype(jnp.bfloat16)


def _gelu(x):
    return 0.5 * x * (1.0 + jnp.tanh(_GELU_K * (x + 0.044715 * (x * x * x))))


def _sigmoid(x):
    return 0.5 * (jnp.tanh(0.5 * x) + 1.0)


def _rms(x, g):
    return x * lax.rsqrt(jnp.mean(x * x, axis=-1, keepdims=True) + EPS) * g


def _lru_log_decay_scale(lam):
    z = -lam
    sp = jnp.maximum(z, 0.0) + jnp.log1p(jnp.exp(-jnp.abs(z)))
    return -LRU_C * sp


def _lru_coeffs(bc, lru_w_ref, lru_b_ref, dscale, after_head=None):
    a_parts, b_parts = [], []
    for h in range(RNN_HEADS):
        if after_head is not None and h > 0:
            after_head(h - 1)
        cs = slice(h * RNN_HD, (h + 1) * RNN_HD)
        bch = bc[:, cs]
        ri = _dot(_bf(bch), lru_w_ref[h])
        r = _sigmoid(ri[:, :RNN_HD] + lru_b_ref[0:1, cs])
        i = _sigmoid(ri[:, RNN_HD:] + lru_b_ref[1:2, cs])
        log_a = dscale[:, cs] * r
        a = jnp.exp(log_a)
        a_parts.append(a)
        b_parts.append(jnp.sqrt((1.0 - a) * (1.0 + a)) * (i * bch))
    return jnp.concatenate(a_parts, axis=1), jnp.concatenate(b_parts, axis=1)


def _pool_project(d_parts, pool_w_ref, pool_scale_ref, w_pa_ref):
    ya = [_dot(_bf(d), pool_w_ref[g]) * pool_scale_ref[:, g * POOL_GW:(g + 1) * POOL_GW]
          for g, d in enumerate(d_parts)]
    return _dot(_bf(jnp.concatenate(ya, axis=1)), w_pa_ref[...])


WEIGHT_STAGE_SLOTS = 3


def _load_weight_as_bf16(src_hbm, layer, dst_ref, rows_per_chunk):
    n_rows, n_cols = dst_ref.shape
    assert n_rows % rows_per_chunk == 0
    n_chunks = n_rows // rows_per_chunk
    n_slots = WEIGHT_STAGE_SLOTS

    def body(stage, sem):
        def copy(i):
            rows = pl.ds(i * rows_per_chunk, rows_per_chunk)
            return pltpu.make_async_copy(src_hbm.at[layer, rows, :], stage.at[i % n_slots],
                                         sem.at[i % n_slots])

        for i in range(min(n_slots - 1, n_chunks)):
            copy(i).start()
        for i in range(n_chunks):
            if i + n_slots - 1 < n_chunks:
                copy(i + n_slots - 1).start()
            copy(i).wait()
            dst_ref[i * rows_per_chunk:(i + 1) * rows_per_chunk, :] = _bf(stage[i % n_slots])

    pl.run_scoped(body, pltpu.VMEM((n_slots, rows_per_chunk, n_cols), jnp.float32),
                  pltpu.SemaphoreType.DMA((n_slots,)))


def _prompt_tile_in_copies(x_hbm, xbuf, sem, tile, slot):
    return [pltpu.make_async_copy(x_hbm.at[s, pl.ds(tile * TILE_T, TILE_T), :],
                                  xbuf.at[slot, :, s, :], sem.at[slot, s]) for s in range(PB)]


def _prompt_tile_out_copies(ybuf, y_hbm, sem, tile, slot):
    return [pltpu.make_async_copy(ybuf.at[slot, :, s, :],
                                  y_hbm.at[s, pl.ds(tile * FFN_T, FFN_T), :], sem.at[slot, s])
            for s in range(PB)]


def _publish_copies(pairs, sem):
    return [pltpu.make_async_copy(src, dst, sem.at[i]) for i, (src, dst) in enumerate(pairs)]


def _build_time_major_spatial_weights(ws_ref, kron_refs):
    row_t = lax.shift_right_logical(lax.broadcasted_iota(jnp.int32, (SUB_R, SUB_T), 0), PB_SHIFT)
    repeat_rows = _bf(row_t == lax.broadcasted_iota(jnp.int32, (SUB_R, SUB_T), 1))
    for k, kron_ref in enumerate(kron_refs):
        n_t, n_r = (k + 1) * SUB_T, (k + 1) * SUB_R
        col_t = lax.shift_right_logical(lax.broadcasted_iota(jnp.int32, (n_t, n_r), 1), PB_SHIFT)
        repeat_cols = _bf(col_t == lax.broadcasted_iota(jnp.int32, (n_t, n_r), 0))
        rr = lax.broadcasted_iota(jnp.int32, (SUB_R, n_r), 0)
        cc = lax.broadcasted_iota(jnp.int32, (SUB_R, n_r), 1)
        same_seq = (rr & (PB - 1)) == (cc & (PB - 1))
        causal = (lax.shift_right_logical(rr, PB_SHIFT) + k * SUB_T) >= lax.shift_right_logical(cc, PB_SHIFT)
        keep = same_seq & causal
        for g in range(CHUNK_GROUPS):
            rows = _dot(repeat_rows, ws_ref[g, k * SUB_T:(k + 1) * SUB_T, 0:n_t])
            full = _dot(_bf(rows), repeat_cols)
            kron_ref[g] = _bf(jnp.where(keep, full, 0.0))


def _prompt_mixer_kernel(x_ref, g1_ref, pool_w_ref, pool_scale_ref, cw_ref, cb_ref,
                         lru_w_ref, lru_b_ref, lam_ref, vg_ref, ws_ref, bs_ref,
                         w_in_ref, w_pa_ref, w_pb_ref, w_pc_ref, w_o_ref,
                         xo_ref, pool_o_ref, rconv_o_ref, h_o_ref,
                         pool_prev, conv_prev, h_carry, *scratch, batch_major_in):
    t = pl.program_id(0)
    n_sub = TILE_T // SUB_T
    kron_refs = scratch[:n_sub]

    @pl.when(t == 0)
    def _():
        pool_prev[...] = jnp.zeros_like(pool_prev)
        conv_prev[...] = jnp.zeros_like(conv_prev)
        h_carry[...] = jnp.zeros_like(h_carry)
        _build_time_major_spatial_weights(ws_ref, kron_refs)

    if batch_major_in:
        xbuf, x_sem = scratch[n_sub:]
        slot = t % 2

        @pl.when(t == 0)
        def _():
            for c in _prompt_tile_in_copies(x_ref, xbuf, x_sem, 0, 0):
                c.start()

        @pl.when(t + 1 < pl.num_programs(0))
        def _():
            for c in _prompt_tile_in_copies(x_ref, xbuf, x_sem, t + 1, 1 - slot):
                c.start()

        for c in _prompt_tile_in_copies(x_ref, xbuf, x_sem, t, slot):
            c.wait()

    dscale = _lru_log_decay_scale(lam_ref[...])
    g1 = g1_ref[...]
    pool_pre = pool_prev[...]
    conv_pre = conv_prev[...]
    h = h_carry[...]
    npre = pool_pre.shape[0]
    v_parts = []

    def load_normed(k):
        if batch_major_in:
            x = xbuf[slot, k * SUB_T:(k + 1) * SUB_T].reshape(SUB_R, D_MODEL)
        else:
            x = x_ref[k * SUB_R:(k + 1) * SUB_R, :]
        return x, _bf(_rms(x, g1))

    x, xn = load_normed(0)
    a_in = _dot(xn, w_in_ref[:, C_POOL:C_POOL + D_POOL])
    b_x = _dot(xn, w_in_ref[:, C_BX:C_BX + D_RNN])
    for k in range(n_sub):
        r0 = k * SUB_R
        proj = lambda c0, n, xn=xn: _dot(xn, w_in_ref[:, c0:c0 + n])

        a_ext = jnp.concatenate([pool_pre, a_in], axis=0)
        pos = t * TILE_T + k * SUB_T + lax.shift_right_logical(
            lax.broadcasted_iota(jnp.int32, (SUB_R, POOL_GW), 0), PB_SHIFT)
        d_parts = []
        for g, w in enumerate(POOL_WINDOWS):
            cs = slice(g * POOL_GW, (g + 1) * POOL_GW)
            s = a_ext[npre - (w - 1) * PB:, cs]
            sh = 1
            while sh < w:
                s = s[sh * PB:] + s[:-sh * PB]
                sh *= 2
            cnt = jnp.minimum(pos + 1, w).astype(jnp.float32)
            d_parts.append(s / cnt - a_in[:, cs])
        pool_pre = a_ext[SUB_R:]
        pa = _pool_project(d_parts, pool_w_ref, pool_scale_ref, w_pa_ref)

        b_ext = jnp.concatenate([conv_pre, b_x], axis=0)
        bc = cb_ref[...] + cw_ref[CONV_WIDTH - 1:CONV_WIDTH, :] * b_x
        for m in range(1, CONV_WIDTH):
            lo = (CONV_WIDTH - 1 - m) * PB
            bc = bc + cw_ref[CONV_WIDTH - 1 - m:CONV_WIDTH - m, :] * b_ext[lo:lo + SUB_R]
        conv_pre = b_ext[SUB_R:]
        late = {}
        late_cols = {0: ('bg', C_BG, D_RNN), 1: ('cu', C_CU, D_CHUNK), 2: ('cv', C_CV, D_CHUNK),
                     3: ('ga', C_GA, D_MODEL), 4: ('gb', C_GB, D_MODEL), 5: ('gc', C_GC, D_MODEL)}

        def issue_late(hd):
            if hd in late_cols:
                name, c0, n = late_cols[hd]
                late[name] = proj(c0, n)

        a_dec, b_inp = _lru_coeffs(bc, lru_w_ref, lru_b_ref, dscale, after_head=issue_late)
        hs = []
        for j in range(SUB_T):
            h = a_dec[j * PB:(j + 1) * PB] * h + b_inp[j * PB:(j + 1) * PB]
            hs.append(h)
        pb = _dot(_bf(_gelu(late['bg']) * jnp.concatenate(hs, axis=0)), w_pb_ref[...])

        u = _gelu(late['cu'])
        v = _rms(_gelu(late['cv']), vg_ref[...])
        v_parts.append(_bf(v))
        v_seen = jnp.concatenate(v_parts, axis=0)
        mix = []
        for g in range(CHUNK_GROUPS):
            cs = slice(g * CHUNK_GW, (g + 1) * CHUNK_GW)
            m = _dot(kron_refs[k][g], v_seen[:, cs])
            mix.append(m + bs_ref[r0:r0 + SUB_R, g:g + 1])
        pc = _dot(_bf(u * jnp.concatenate(mix, axis=1)), w_pc_ref[...])

        more = k + 1 < n_sub
        if more:
            x_nxt, xn_nxt = load_normed(k + 1)
        merged = _sigmoid(late['ga']) * pa
        if more:
            a_nxt = _dot(xn_nxt, w_in_ref[:, C_POOL:C_POOL + D_POOL])
        merged = merged + _sigmoid(late['gb']) * pb
        if more:
            b_nxt = _dot(xn_nxt, w_in_ref[:, C_BX:C_BX + D_RNN])
        merged = merged + _sigmoid(late['gc']) * pc
        xo_ref[r0:r0 + SUB_R, :] = x + _dot(_bf(merged), w_o_ref[...])
        if more:
            x, xn, a_in, b_x = x_nxt, xn_nxt, a_nxt, b_nxt

    pool_prev[...] = pool_pre
    conv_prev[...] = conv_pre
    h_carry[...] = h
    pool_o_ref[...] = pool_pre
    rconv_o_ref[...] = conv_pre
    h_o_ref[...] = h


def _layer_spec(a, l):
    nd = a.ndim
    return pl.BlockSpec((None,) + a.shape[1:], lambda *_: (l,) + (0,) * (nd - 1),
                        pipeline_mode=pl.Buffered(1))


def _whole_spec(a):
    nd = a.ndim
    return pl.BlockSpec(a.shape, lambda *_: (0,) * nd, pipeline_mode=pl.Buffered(1))


_MIXER_SMALL = ('g1', 'pool_w', 'pool_scale', 'cw', 'cb', 'lru_w', 'lru_b', 'lam', 'vg')
_MIXER_BIG = ('w_in', 'w_pa', 'w_pb', 'w_pc', 'w_o')
_FFN_SMALL = ('g2', 'fcw', 'fcb')
_FFN_BIG = ('wg', 'wu', 'wd')
_ANY = pl.BlockSpec(memory_space=pl.ANY)


def _prompt_mixer(x, w, l, batch_major_in):
    f32, bf16 = jnp.float32, jnp.bfloat16
    scratch_in = []
    if batch_major_in:
        n_rows = x.shape[0] * x.shape[1]
        x_spec = _ANY
        scratch_in = [pltpu.VMEM((2, TILE_T, PB, D_MODEL), f32), pltpu.SemaphoreType.DMA((2, PB))]
    else:
        n_rows = x.shape[0]
        x_spec = pl.BlockSpec((TILE_R, D_MODEL), lambda t: (t, 0))
    weights = [w[k] for k in _MIXER_SMALL + ('ws', 'bs_tm')] + [w[k + '_bf'] for k in _MIXER_BIG]
    pool_rows = (POOL_STATE + 1) * PB
    conv_rows = (CONV_WIDTH - 1) * PB
    return pl.pallas_call(
        functools.partial(_prompt_mixer_kernel, batch_major_in=batch_major_in),
        grid=(n_rows // TILE_R,),
        in_specs=[x_spec] + [_layer_spec(a, l) for a in weights],
        out_specs=[pl.BlockSpec((TILE_R, D_MODEL), lambda t: (t, 0)),
                   pl.BlockSpec((pool_rows, D_POOL), lambda t: (0, 0)),
                   pl.BlockSpec((conv_rows, D_RNN), lambda t: (0, 0)),
                   pl.BlockSpec((PB, D_RNN), lambda t: (0, 0))],
        out_shape=[jax.ShapeDtypeStruct((n_rows, D_MODEL), f32),
                   jax.ShapeDtypeStruct((pool_rows, D_POOL), f32),
                   jax.ShapeDtypeStruct((conv_rows, D_RNN), f32),
                   jax.ShapeDtypeStruct((PB, D_RNN), f32)],
        scratch_shapes=[pltpu.VMEM((pool_rows, D_POOL), f32),
                        pltpu.VMEM((conv_rows, D_RNN), f32),
                        pltpu.VMEM((PB, D_RNN), f32)]
                       + [pltpu.VMEM((CHUNK_GROUPS, SUB_R, (k + 1) * SUB_R), bf16)
                          for k in range(TILE_T // SUB_T)]
                       + scratch_in,
        compiler_params=pltpu.CompilerParams(dimension_semantics=("arbitrary",),
                                             vmem_limit_bytes=VMEM_LIMIT),
        name="prompt_mixer",
    )(x, *weights)


def _prompt_ffn_kernel(x_ref, x_next_ref, g2_ref, fcw_ref, fcb_ref, gf_ref, wg_hbm, wu_hbm, wd_hbm,
                       xo_ref, ffn_o_ref, wg_pub, wu_pub, wd_pub,
                       g_prev, wg_ref, wu_ref, wd_ref, pub_sem, xn_scr, *out_scratch,
                       layer, last_layer):
    t = pl.program_id(0)
    n_steps = pl.num_programs(0)
    publish = _publish_copies([(wg_ref, wg_pub), (wu_ref, wu_pub), (wd_ref, wd_pub)], pub_sem)

    @pl.when(t == 0)
    def _():
        g_prev[...] = jnp.zeros_like(g_prev)
        _load_weight_as_bf16(wg_hbm, layer, wg_ref, 256)
        _load_weight_as_bf16(wu_hbm, layer, wu_ref, 256)
        _load_weight_as_bf16(wd_hbm, layer, wd_ref, 768)
        for c in publish:
            c.start()
        xn_scr[0] = _bf(_rms(x_ref[...], g2_ref[...]))

    @pl.when(t == pl.num_programs(0) - 1)
    def _():
        for c in publish:
            c.wait()

    xn_slot = t % 2
    x = x_ref[...]
    xn = xn_scr[xn_slot]
    acc = x
    npre = (FFN_CONV - 1) * PB
    n_chunks = D_FF // FFN_C
    col = lambda c: slice(c * FFN_C, (c + 1) * FFN_C)
    up_proj = lambda c: (_dot(xn, wg_ref[:, col(c)]), _dot(xn, wu_ref[:, col(c)]))
    nxt = up_proj(0)
    for c in range(n_chunks):
        cs = col(c)
        g_pre, up = nxt
        if c + 1 < n_chunks:
            nxt = up_proj(c + 1)
        g_ext = jnp.concatenate([g_prev[:, cs], g_pre], axis=0)
        y = fcb_ref[:, cs] + fcw_ref[FFN_CONV - 1:FFN_CONV, cs] * g_pre
        for m in range(1, FFN_CONV):
            lo = npre - m * PB
            y = y + fcw_ref[FFN_CONV - 1 - m:FFN_CONV - m, cs] * g_ext[lo:lo + FFN_R]
        g_prev[:, cs] = g_ext[FFN_R:]
        h = _gelu(y) * up
        acc = acc + _dot(_bf(h), wd_ref[cs, :])
        if c == 0:
            xn_scr[1 - xn_slot] = _bf(_rms(x_next_ref[...], g2_ref[...]))
    if last_layer:
        ybuf, y_sem = out_scratch
        slot = t % 2
        out_copies = functools.partial(_prompt_tile_out_copies, ybuf, xo_ref, y_sem)

        @pl.when(t >= 2)
        def _():
            for c in out_copies(t - 2, slot):
                c.wait()

        ybuf[slot] = _rms(acc, gf_ref[...]).reshape(FFN_T, PB, D_MODEL)
        for c in out_copies(t, slot):
            c.start()

        @pl.when(t == n_steps - 1)
        def _():
            @pl.when(t >= 1)
            def _():
                for c in out_copies(t - 1, 1 - slot):
                    c.wait()
            for c in out_copies(t, slot):
                c.wait()
    else:
        xo_ref[...] = acc
    ffn_o_ref[...] = g_prev[...]


def _prompt_ffn(x_tm, w, l, last_layer):
    n_rows = x_tm.shape[0]
    row_spec = pl.BlockSpec((FFN_R, D_MODEL), lambda t: (t, 0))
    pre_rows = (FFN_CONV - 1) * PB
    out_scratch = []
    if last_layer:
        out_spec = _ANY
        out_shape = jax.ShapeDtypeStruct((PB, n_rows // PB, D_MODEL), jnp.float32)
        out_scratch = [pltpu.VMEM((2, FFN_T, PB, D_MODEL), jnp.float32),
                       pltpu.SemaphoreType.DMA((2, PB))]
    else:
        out_spec, out_shape = row_spec, jax.ShapeDtypeStruct((n_rows, D_MODEL), jnp.float32)
    small = [w[k] for k in _FFN_SMALL]
    big = [w[k] for k in _FFN_BIG]
    bf16 = jnp.bfloat16
    n_tiles = n_rows // FFN_R
    next_row_spec = pl.BlockSpec((FFN_R, D_MODEL), lambda t: (jnp.minimum(t + 1, n_tiles - 1), 0))
    outs = pl.pallas_call(
        functools.partial(_prompt_ffn_kernel, layer=l, last_layer=last_layer),
        grid=(n_tiles,),
        in_specs=[row_spec, next_row_spec] + [_layer_spec(a, l) for a in small]
                 + [_whole_spec(w['gf'])] + [_ANY] * len(big),
        out_specs=[out_spec, pl.BlockSpec((pre_rows, D_FF), lambda t: (0, 0))] + [_ANY] * len(big),
        out_shape=[out_shape, jax.ShapeDtypeStruct((pre_rows, D_FF), jnp.float32)]
                  + [jax.ShapeDtypeStruct(a.shape[1:], bf16) for a in big],
        scratch_shapes=[pltpu.VMEM((pre_rows, D_FF), jnp.float32)]
                       + [pltpu.VMEM(a.shape[1:], bf16) for a in big]
                       + [pltpu.SemaphoreType.DMA((len(big),)),
                          pltpu.VMEM((2, FFN_R, D_MODEL), bf16)] + out_scratch,
        compiler_params=pltpu.CompilerParams(dimension_semantics=("arbitrary",),
                                             vmem_limit_bytes=VMEM_LIMIT),
        name="prompt_ffn",
    )(x_tm, x_tm, *small, w['gf'], *big)
    return outs[:2], dict(zip(_FFN_BIG, outs[2:]))


def _sample_mixer_kernel(x_ref, pool_st_ref, rconv_st_ref, h0_ref,
                         g1_ref, pool_w_ref, pool_scale_ref, cw_ref, cb_ref,
                         lru_w_ref, lru_b_ref, lam_ref, vg_ref, ws0_ref, bs0_ref,
                         w_in_ref, w_pa_ref, w_pb_ref, w_pc_ref, w_o_ref,
                         xo_ref, pool_o_ref, rconv_o_ref, h_o_ref, v_o_ref):
    x = x_ref[...]
    xn = _bf(_rms(x, g1_ref[...]))

    a_in = _dot(xn, w_in_ref[:, C_POOL:C_POOL + D_POOL])
    d_parts = []
    for g, w in enumerate(POOL_WINDOWS):
        s = a_in[:, g * POOL_GW:(g + 1) * POOL_GW]
        for m in range(1, w):
            s = s + pool_st_ref[POOL_STATE - m, :, g * POOL_GW:(g + 1) * POOL_GW]
        cnt = float(min(PAST_LEN + 1, w))
        d_parts.append(s / cnt - a_in[:, g * POOL_GW:(g + 1) * POOL_GW])
    pool_o_ref[:POOL_STATE - 1] = pool_st_ref[1:]
    pool_o_ref[POOL_STATE - 1] = a_in
    pa = _pool_project(d_parts, pool_w_ref, pool_scale_ref, w_pa_ref)

    b_x = _dot(xn, w_in_ref[:, C_BX:C_BX + D_RNN])
    bc = cb_ref[...] + cw_ref[CONV_WIDTH - 1:CONV_WIDTH, :] * b_x
    for m in range(CONV_WIDTH - 1):
        bc = bc + cw_ref[m:m + 1, :] * rconv_st_ref[m]
    rconv_o_ref[:CONV_WIDTH - 2] = rconv_st_ref[1:]
    rconv_o_ref[CONV_WIDTH - 2] = b_x
    a_dec, b_inp = _lru_coeffs(bc, lru_w_ref, lru_b_ref, _lru_log_decay_scale(lam_ref[...]))
    h = a_dec * h0_ref[...] + b_inp
    h_o_ref[...] = h
    b_gate = _dot(xn, w_in_ref[:, C_BG:C_BG + D_RNN])
    pb = _dot(_bf(_gelu(b_gate) * h), w_pb_ref[...])

    u = _gelu(_dot(xn, w_in_ref[:, C_CU:C_CU + D_CHUNK]))
    v = _rms(_gelu(_dot(xn, w_in_ref[:, C_CV:C_CV + D_CHUNK])), vg_ref[...])
    v_o_ref[...] = v
    mix = ws0_ref[...] * v + bs0_ref[...]
    pc = _dot(_bf(u * mix), w_pc_ref[...])

    merged = _sigmoid(_dot(xn, w_in_ref[:, C_GA:C_GA + D_MODEL])) * pa
    merged = merged + _sigmoid(_dot(xn, w_in_ref[:, C_GB:C_GB + D_MODEL])) * pb
    merged = merged + _sigmoid(_dot(xn, w_in_ref[:, C_GC:C_GC + D_MODEL])) * pc
    xo_ref[...] = x + _dot(_bf(merged), w_o_ref[...])


def _sample_mixer(xs, pool_st, rconv_st, h0, w, l):
    n = xs.shape[0]
    f32 = jnp.float32
    acts = [xs, pool_st, rconv_st, h0]
    weights = [w[k] for k in _MIXER_SMALL + ('ws0', 'bs0')] + [w[k + '_bf'] for k in _MIXER_BIG]
    out_shape = [jax.ShapeDtypeStruct((n, D_MODEL), f32),
                 jax.ShapeDtypeStruct((POOL_STATE, n, D_POOL), f32),
                 jax.ShapeDtypeStruct((CONV_WIDTH - 1, n, D_RNN), f32),
                 jax.ShapeDtypeStruct((n, D_RNN), f32),
                 jax.ShapeDtypeStruct((n, D_CHUNK), f32)]
    return pl.pallas_call(
        _sample_mixer_kernel,
        grid=(1,),
        in_specs=[_whole_spec(xs)] + [_layer_spec(a, l) for a in acts[1:] + weights],
        out_specs=[pl.BlockSpec(s.shape, lambda t, nd=len(s.shape): (0,) * nd) for s in out_shape],
        out_shape=out_shape,
        compiler_params=pltpu.CompilerParams(dimension_semantics=("arbitrary",),
                                             vmem_limit_bytes=VMEM_LIMIT),
        name="sample_mixer",
    )(*acts, *weights)


def _sample_ffn_kernel(x_ref, st_ref, g2_ref, fcw_ref, fcb_ref, gf_ref, wg_ref, wu_ref, wd_ref,
                       xo_ref, st_o_ref, *, last_layer):
    x = x_ref[...]
    xn = _bf(_rms(x, g2_ref[...]))
    g_pre = _dot(xn, wg_ref[...])
    y = fcb_ref[...] + fcw_ref[FFN_CONV - 1:FFN_CONV, :] * g_pre
    for m in range(FFN_CONV - 1):
        y = y + fcw_ref[m:m + 1, :] * st_ref[m]
    st_o_ref[:FFN_CONV - 2] = st_ref[1:]
    st_o_ref[FFN_CONV - 2] = g_pre
    h = _gelu(y) * _dot(xn, wu_ref[...])
    out = x + _dot(_bf(h), wd_ref[...])
    if last_layer:
        out = _rms(out, gf_ref[...])
    xo_ref[...] = out


def _sample_ffn(xs, st, w, w_bf16, l, last_layer):
    n = xs.shape[0]
    small = [w[k] for k in _FFN_SMALL]
    big = [w_bf16[k] for k in _FFN_BIG]
    out_shape = [jax.ShapeDtypeStruct((n, D_MODEL), jnp.float32),
                 jax.ShapeDtypeStruct((FFN_CONV - 1, n, D_FF), jnp.float32)]
    return pl.pallas_call(
        functools.partial(_sample_ffn_kernel, last_layer=last_layer),
        grid=(1,),
        in_specs=[_whole_spec(xs), _layer_spec(st, l)] + [_layer_spec(a, l) for a in small]
                 + [_whole_spec(w['gf'])] + [_whole_spec(a) for a in big],
        out_specs=[pl.BlockSpec(s.shape, lambda t, nd=len(s.shape): (0,) * nd) for s in out_shape],
        out_shape=out_shape,
        compiler_params=pltpu.CompilerParams(dimension_semantics=("arbitrary",),
                                             vmem_limit_bytes=VMEM_LIMIT),
        name="sample_ffn",
    )(xs, st, *small, w['gf'], *big)


def _prepare_weights(p):
    bf16 = jnp.bfloat16
    depth = p['w_in'].shape[0]
    row = lambda a: a.reshape(depth, 1, -1)
    return {
        'g1': row(p['norm1_g']),
        'pool_w': p['pool_w'].astype(bf16), 'pool_scale': row(p['pool_scale']),
        'cw': p['rnn_conv_w'], 'cb': row(p['rnn_conv_b']),
        'lru_w': jnp.concatenate([p['lru_wa'], p['lru_wx']], axis=-1).astype(bf16),
        'lru_b': jnp.stack([p['lru_ba'], p['lru_bx']], axis=1),
        'lam': row(p['lru_lambda']), 'vg': row(p['chunk_vnorm_g']),
        'ws': p['chunk_ws'].astype(bf16),
        'bs_tm': jnp.repeat(jnp.swapaxes(p['chunk_bs'], 1, 2), PB, axis=1),
        'ws0': row(jnp.repeat(p['chunk_ws'][:, :, 0, 0], CHUNK_GW, axis=1)),
        'bs0': row(jnp.repeat(p['chunk_bs'][:, :, 0], CHUNK_GW, axis=1)),
        'w_in_bf': p['w_in'].astype(bf16), 'w_pa_bf': p['w_pa'].astype(bf16),
        'w_pb_bf': p['w_pb'].astype(bf16), 'w_pc_bf': p['w_pc'].astype(bf16),
        'w_o_bf': p['w_o'].astype(bf16),
        'g2': row(p['norm2_g']), 'wg': p['ffn_wg'], 'wu': p['ffn_wu'], 'fcw': p['ffn_conv_w'],
        'fcb': row(p['ffn_conv_b']), 'wd': p['ffn_wd'],
        'gf': p['final_norm_g'].reshape(1, -1),
    }


def _batch_major(rows_tm, n):
    return jnp.transpose(rows_tm.reshape(-1, PB, rows_tm.shape[-1])[-n:], (1, 0, 2))


def kernel(x_prompt, x_sample, state_pool, state_rnn_conv, state_rnn_h, state_ffn_conv, norm1_g, w_in, pool_w, pool_scale, rnn_conv_w, rnn_conv_b, lru_wa, lru_ba, lru_wx, lru_bx, lru_lambda, chunk_vnorm_g, chunk_ws, chunk_bs, w_pa, w_pb, w_pc, w_o, norm2_g, ffn_wg, ffn_wu, ffn_conv_w, ffn_conv_b, ffn_wd, final_norm_g):
    p = dict(norm1_g=norm1_g, w_in=w_in, pool_w=pool_w, pool_scale=pool_scale,
             rnn_conv_w=rnn_conv_w, rnn_conv_b=rnn_conv_b, lru_wa=lru_wa, lru_ba=lru_ba,
             lru_wx=lru_wx, lru_bx=lru_bx, lru_lambda=lru_lambda, chunk_vnorm_g=chunk_vnorm_g,
             chunk_ws=chunk_ws, chunk_bs=chunk_bs, w_pa=w_pa, w_pb=w_pb, w_pc=w_pc, w_o=w_o,
             norm2_g=norm2_g, ffn_wg=ffn_wg, ffn_wu=ffn_wu, ffn_conv_w=ffn_conv_w,
             ffn_conv_b=ffn_conv_b, ffn_wd=ffn_wd, final_norm_g=final_norm_g)
    depth = w_in.shape[0]
    bp, tp, _ = x_prompt.shape
    ns = x_sample.shape[0]
    assert bp == PB and tp % TILE_T == 0 and x_sample.shape[1] == 1
    w = _prepare_weights(p)

    xp = x_prompt
    xs = x_sample.reshape(ns, D_MODEL)
    pool_st = jnp.swapaxes(state_pool, 1, 2)
    rconv_st = jnp.swapaxes(state_rnn_conv, 1, 2)
    ffn_st = jnp.swapaxes(state_ffn_conv, 1, 2)
    outs = {k: [] for k in ('pool_p', 'pool_s', 'rc_p', 'rc_s', 'h_p', 'h_s', 'ff_p', 'ff_s', 'cv_s')}
    for l in range(depth):
        last = l == depth - 1
        xp, pool_tm, rconv_tm, h_p = _prompt_mixer(xp, w, l, batch_major_in=(l == 0))
        (xp, ffn_tm), ffn_bf16 = _prompt_ffn(xp, w, l, last)
        xs, pool_s, rconv_s, h_s, v_s = _sample_mixer(xs, pool_st, rconv_st, state_rnn_h, w, l)
        xs, ffn_s = _sample_ffn(xs, ffn_st, w, ffn_bf16, l, last)
        outs['pool_p'].append(_batch_major(pool_tm, POOL_STATE))
        outs['rc_p'].append(_batch_major(rconv_tm, CONV_WIDTH - 1))
        outs['h_p'].append(h_p)
        outs['ff_p'].append(_batch_major(ffn_tm, FFN_CONV - 1))
        outs['pool_s'].append(jnp.swapaxes(pool_s, 0, 1))
        outs['rc_s'].append(jnp.swapaxes(rconv_s, 0, 1))
        outs['h_s'].append(h_s)
        outs['ff_s'].append(jnp.swapaxes(ffn_s, 0, 1))
        outs['cv_s'].append(v_s.reshape(ns, 1, D_CHUNK))
    st = jnp.stack
    return (xp, xs.reshape(ns, 1, D_MODEL), st(outs['pool_p']), st(outs['pool_s']),
            st(outs['rc_p']), st(outs['rc_s']), st(outs['h_p']), st(outs['h_s']),
            st(outs['ff_p']), st(outs['ff_s']), st(outs['cv_s']))
```

```python
import functools
import math

import jax
import jax.numpy as jnp
from jax import lax
from jax.experimental import pallas as pl
from jax.experimental.pallas import tpu as pltpu

D_MODEL = 1024
D_POOL = 512
POOL_WINDOWS = (2, 4, 8, 16)
POOL_GW = 128
POOL_STATE = 15
D_RNN = 1024
RNN_HEADS = 8
RNN_HD = 128
CONV_WIDTH = 4
LRU_C = 8.0
D_CHUNK = 512
CHUNK = 128
CHUNK_GROUPS = 4
CHUNK_GW = 128
D_FF = 3072
FFN_CONV = 3
EPS = 1e-6
PAST_LEN = 16384

C_POOL = 0
C_BX = D_POOL
C_BG = C_BX + D_RNN
C_CU = C_BG + D_RNN
C_CV = C_CU + D_CHUNK
C_GA = C_CV + D_CHUNK
C_GB = C_GA + D_MODEL
C_GC = C_GB + D_MODEL

PB = 8
PB_SHIFT = PB.bit_length() - 1
TILE_T = CHUNK
TILE_R = TILE_T * PB
SUB_T = 64
SUB_R = SUB_T * PB
FFN_R = 512
FFN_T = FFN_R // PB
FFN_C = 1536
VMEM_LIMIT = 60 * 1024 * 1024

_GELU_K = math.sqrt(2.0 / math.pi)


def _dot(a, b):
    return jnp.dot(a, b, preferred_element_type=jnp.float32)


def _bf(x):
    return x.astype(jnp.bfloat16)


def _gelu(x):
    return 0.5 * x * (1.0 + jnp.tanh(_GELU_K * (x + 0.044715 * (x * x * x))))


def _sigmoid(x):
    return 0.5 * (jnp.tanh(0.5 * x) + 1.0)


def _rms(x, g):
    return x * lax.rsqrt(jnp.mean(x * x, axis=-1, keepdims=True) + EPS) * g


def _lru_log_decay_scale(lam):
    z = -lam
    sp = jnp.maximum(z, 0.0) + jnp.log1p(jnp.exp(-jnp.abs(z)))
    return -LRU_C * sp


def _lru_coeffs(bc, lru_w_ref, lru_b_ref, dscale, after_head=None):
    a_parts, b_parts = [], []
    for h in range(RNN_HEADS):
        if after_head is not None and h > 0:
            after_head(h - 1)
        cs = slice(h * RNN_HD, (h + 1) * RNN_HD)
        bch = bc[:, cs]
        ri = _dot(_bf(bch), lru_w_ref[h])
        r = _sigmoid(ri[:, :RNN_HD] + lru_b_ref[0:1, cs])
        i = _sigmoid(ri[:, RNN_HD:] + lru_b_ref[1:2, cs])
        log_a = dscale[:, cs] * r
        a = jnp.exp(log_a)
        a_parts.append(a)
        b_parts.append(jnp.sqrt((1.0 - a) * (1.0 + a)) * (i * bch))
    return jnp.concatenate(a_parts, axis=1), jnp.concatenate(b_parts, axis=1)


def _pool_project(d_parts, pool_w_ref, pool_scale_ref, w_pa_ref):
    ya = [_dot(_bf(d), pool_w_ref[g]) * pool_scale_ref[:, g * POOL_GW:(g + 1) * POOL_GW]
          for g, d in enumerate(d_parts)]
    return _dot(_bf(jnp.concatenate(ya, axis=1)), w_pa_ref[...])


WEIGHT_STAGE_SLOTS = 3


def _load_weight_as_bf16(src_hbm, layer, dst_ref, rows_per_chunk):
    n_rows, n_cols = dst_ref.shape
    assert n_rows % rows_per_chunk == 0
    n_chunks = n_rows // rows_per_chunk
    n_slots = WEIGHT_STAGE_SLOTS

    def body(stage, sem):
        def copy(i):
            rows = pl.ds(i * rows_per_chunk, rows_per_chunk)
            return pltpu.make_async_copy(src_hbm.at[layer, rows, :], stage.at[i % n_slots],
                                         sem.at[i % n_slots])

        for i in range(min(n_slots - 1, n_chunks)):
            copy(i).start()
        for i in range(n_chunks):
            if i + n_slots - 1 < n_chunks:
                copy(i + n_slots - 1).start()
            copy(i).wait()
            dst_ref[i * rows_per_chunk:(i + 1) * rows_per_chunk, :] = _bf(stage[i % n_slots])

    pl.run_scoped(body, pltpu.VMEM((n_slots, rows_per_chunk, n_cols), jnp.float32),
                  pltpu.SemaphoreType.DMA((n_slots,)))


def _prompt_tile_in_copies(x_hbm, xbuf, sem, tile, slot):
    return [pltpu.make_async_copy(x_hbm.at[s, pl.ds(tile * TILE_T, TILE_T), :],
                                  xbuf.at[slot, :, s, :], sem.at[slot, s]) for s in range(PB)]


def _prompt_tile_out_copies(ybuf, y_hbm, sem, tile, slot):
    return [pltpu.make_async_copy(ybuf.at[slot, :, s, :],
                                  y_hbm.at[s, pl.ds(tile * FFN_T, FFN_T), :], sem.at[slot, s])
            for s in range(PB)]


def _build_time_major_spatial_weights(ws_ref, kron_refs):
    row_t = lax.shift_right_logical(lax.broadcasted_iota(jnp.int32, (SUB_R, SUB_T), 0), PB_SHIFT)
    repeat_rows = _bf(row_t == lax.broadcasted_iota(jnp.int32, (SUB_R, SUB_T), 1))
    for k, kron_ref in enumerate(kron_refs):
        n_t, n_r = (k + 1) * SUB_T, (k + 1) * SUB_R
        col_t = lax.shift_right_logical(lax.broadcasted_iota(jnp.int32, (n_t, n_r), 1), PB_SHIFT)
        repeat_cols = _bf(col_t == lax.broadcasted_iota(jnp.int32, (n_t, n_r), 0))
        rr = lax.broadcasted_iota(jnp.int32, (SUB_R, n_r), 0)
        cc = lax.broadcasted_iota(jnp.int32, (SUB_R, n_r), 1)
        same_seq = (rr & (PB - 1)) == (cc & (PB - 1))
        causal = (lax.shift_right_logical(rr, PB_SHIFT) + k * SUB_T) >= lax.shift_right_logical(cc, PB_SHIFT)
        keep = same_seq & causal
        for g in range(CHUNK_GROUPS):
            rows = _dot(repeat_rows, ws_ref[g, k * SUB_T:(k + 1) * SUB_T, 0:n_t])
            full = _dot(_bf(rows), repeat_cols)
            kron_ref[g] = _bf(jnp.where(keep, full, 0.0))


def _prompt_mixer_kernel(x_ref, g1_ref, pool_w_ref, pool_scale_ref, cw_ref, cb_ref,
                         lru_w_ref, lru_b_ref, lam_ref, vg_ref, ws_ref, bs_ref,
                         w_in_ref, w_pa_ref, w_pb_ref, w_pc_ref, w_o_ref,
                         xo_ref, pool_o_ref, rconv_o_ref, h_o_ref,
                         pool_prev, conv_prev, h_carry, *scratch, batch_major_in):
    t = pl.program_id(0)
    n_sub = TILE_T // SUB_T
    kron_refs = scratch[:n_sub]

    @pl.when(t == 0)
    def _():
        pool_prev[...] = jnp.zeros_like(pool_prev)
        conv_prev[...] = jnp.zeros_like(conv_prev)
        h_carry[...] = jnp.zeros_like(h_carry)
        _build_time_major_spatial_weights(ws_ref, kron_refs)

    if batch_major_in:
        xbuf, x_sem = scratch[n_sub:]
        slot = t % 2

        @pl.when(t == 0)
        def _():
            for c in _prompt_tile_in_copies(x_ref, xbuf, x_sem, 0, 0):
                c.start()

        @pl.when(t + 1 < pl.num_programs(0))
        def _():
            for c in _prompt_tile_in_copies(x_ref, xbuf, x_sem, t + 1, 1 - slot):
                c.start()

        for c in _prompt_tile_in_copies(x_ref, xbuf, x_sem, t, slot):
            c.wait()

    dscale = _lru_log_decay_scale(lam_ref[...])
    g1 = g1_ref[...]
    pool_pre = pool_prev[...]
    conv_pre = conv_prev[...]
    h = h_carry[...]
    npre = pool_pre.shape[0]
    v_parts = []

    def load_normed(k):
        if batch_major_in:
            x = xbuf[slot, k * SUB_T:(k + 1) * SUB_T].reshape(SUB_R, D_MODEL)
        else:
            x = x_ref[k * SUB_R:(k + 1) * SUB_R, :]
        return x, _bf(_rms(x, g1))

    x, xn = load_normed(0)
    a_in = _dot(xn, w_in_ref[:, C_POOL:C_POOL + D_POOL])
    b_x = _dot(xn, w_in_ref[:, C_BX:C_BX + D_RNN])
    for k in range(n_sub):
        r0 = k * SUB_R
        proj = lambda c0, n, xn=xn: _dot(xn, w_in_ref[:, c0:c0 + n])

        a_ext = jnp.concatenate([pool_pre, a_in], axis=0)
        pos = t * TILE_T + k * SUB_T + lax.shift_right_logical(
            lax.broadcasted_iota(jnp.int32, (SUB_R, POOL_GW), 0), PB_SHIFT)
        d_parts = []
        for g, w in enumerate(POOL_WINDOWS):
            cs = slice(g * POOL_GW, (g + 1) * POOL_GW)
            s = a_ext[npre - (w - 1) * PB:, cs]
            sh = 1
            while sh < w:
                s = s[sh * PB:] + s[:-sh * PB]
                sh *= 2
            cnt = jnp.minimum(pos + 1, w).astype(jnp.float32)
            d_parts.append(s / cnt - a_in[:, cs])
        pool_pre = a_ext[SUB_R:]
        pa = _pool_project(d_parts, pool_w_ref, pool_scale_ref, w_pa_ref)

        b_ext = jnp.concatenate([conv_pre, b_x], axis=0)
        bc = cb_ref[...] + cw_ref[CONV_WIDTH - 1:CONV_WIDTH, :] * b_x
        for m in range(1, CONV_WIDTH):
            lo = (CONV_WIDTH - 1 - m) * PB
            bc = bc + cw_ref[CONV_WIDTH - 1 - m:CONV_WIDTH - m, :] * b_ext[lo:lo + SUB_R]
        conv_pre = b_ext[SUB_R:]
        late = {}
        late_cols = {0: ('bg', C_BG, D_RNN), 1: ('cu', C_CU, D_CHUNK), 2: ('cv', C_CV, D_CHUNK),
                     3: ('ga', C_GA, D_MODEL), 4: ('gb', C_GB, D_MODEL), 5: ('gc', C_GC, D_MODEL)}

        def issue_late(hd):
            if hd in late_cols:
                name, c0, n = late_cols[hd]
                late[name] = proj(c0, n)

        a_dec, b_inp = _lru_coeffs(bc, lru_w_ref, lru_b_ref, dscale, after_head=issue_late)
        hs = []
        for j in range(SUB_T):
            h = a_dec[j * PB:(j + 1) * PB] * h + b_inp[j * PB:(j + 1) * PB]
            hs.append(h)
        pb = _dot(_bf(_gelu(late['bg']) * jnp.concatenate(hs, axis=0)), w_pb_ref[...])

        u = _gelu(late['cu'])
        v = _rms(_gelu(late['cv']), vg_ref[...])
        v_parts.append(_bf(v))
        v_seen = jnp.concatenate(v_parts, axis=0)
        mix = []
        for g in range(CHUNK_GROUPS):
            cs = slice(g * CHUNK_GW, (g + 1) * CHUNK_GW)
            m = _dot(kron_refs[k][g], v_seen[:, cs])
            mix.append(m + bs_ref[r0:r0 + SUB_R, g:g + 1])
        pc = _dot(_bf(u * jnp.concatenate(mix, axis=1)), w_pc_ref[...])

        more = k + 1 < n_sub
        if more:
            x_nxt, xn_nxt = load_normed(k + 1)
        merged = _sigmoid(late['ga']) * pa
        if more:
            a_nxt = _dot(xn_nxt, w_in_ref[:, C_POOL:C_POOL + D_POOL])
        merged = merged + _sigmoid(late['gb']) * pb
        if more:
            b_nxt = _dot(xn_nxt, w_in_ref[:, C_BX:C_BX + D_RNN])
        merged = merged + _sigmoid(late['gc']) * pc
        xo_ref[r0:r0 + SUB_R, :] = x + _dot(_bf(merged), w_o_ref[...])
        if more:
            x, xn, a_in, b_x = x_nxt, xn_nxt, a_nxt, b_nxt

    pool_prev[...] = pool_pre
    conv_prev[...] = conv_pre
    h_carry[...] = h
    pool_o_ref[...] = pool_pre
    rconv_o_ref[...] = conv_pre
    h_o_ref[...] = h


def _layer_spec(a, l):
    nd = a.ndim
    return pl.BlockSpec((None,) + a.shape[1:], lambda *_: (l,) + (0,) * (nd - 1),
                        pipeline_mode=pl.Buffered(1))


def _whole_spec(a):
    nd = a.ndim
    return pl.BlockSpec(a.shape, lambda *_: (0,) * nd, pipeline_mode=pl.Buffered(1))


_MIXER_SMALL = ('g1', 'pool_w', 'pool_scale', 'cw', 'cb', 'lru_w', 'lru_b', 'lam', 'vg')
_MIXER_BIG = ('w_in', 'w_pa', 'w_pb', 'w_pc', 'w_o')
_FFN_SMALL = ('g2', 'fcw', 'fcb')
_FFN_BIG = ('wg', 'wu', 'wd')
_ANY = pl.BlockSpec(memory_space=pl.ANY)


def _prompt_mixer(x, w, l, batch_major_in):
    f32, bf16 = jnp.float32, jnp.bfloat16
    scratch_in = []
    if batch_major_in:
        n_rows = x.shape[0] * x.shape[1]
        x_spec = _ANY
        scratch_in = [pltpu.VMEM((2, TILE_T, PB, D_MODEL), f32), pltpu.SemaphoreType.DMA((2, PB))]
    else:
        n_rows = x.shape[0]
        x_spec = pl.BlockSpec((TILE_R, D_MODEL), lambda t: (t, 0))
    weights = [w[k] for k in _MIXER_SMALL + ('ws', 'bs_tm')] + [w[k + '_bf'] for k in _MIXER_BIG]
    pool_rows = (POOL_STATE + 1) * PB
    conv_rows = (CONV_WIDTH - 1) * PB
    return pl.pallas_call(
        functools.partial(_prompt_mixer_kernel, batch_major_in=batch_major_in),
        grid=(n_rows // TILE_R,),
        in_specs=[x_spec] + [_layer_spec(a, l) for a in weights],
        out_specs=[pl.BlockSpec((TILE_R, D_MODEL), lambda t: (t, 0)),
                   pl.BlockSpec((pool_rows, D_POOL), lambda t: (0, 0)),
                   pl.BlockSpec((conv_rows, D_RNN), lambda t: (0, 0)),
                   pl.BlockSpec((PB, D_RNN), lambda t: (0, 0))],
        out_shape=[jax.ShapeDtypeStruct((n_rows, D_MODEL), f32),
                   jax.ShapeDtypeStruct((pool_rows, D_POOL), f32),
                   jax.ShapeDtypeStruct((conv_rows, D_RNN), f32),
                   jax.ShapeDtypeStruct((PB, D_RNN), f32)],
        scratch_shapes=[pltpu.VMEM((pool_rows, D_POOL), f32),
                        pltpu.VMEM((conv_rows, D_RNN), f32),
                        pltpu.VMEM((PB, D_RNN), f32)]
                       + [pltpu.VMEM((CHUNK_GROUPS, SUB_R, (k + 1) * SUB_R), bf16)
                          for k in range(TILE_T // SUB_T)]
                       + scratch_in,
        compiler_params=pltpu.CompilerParams(dimension_semantics=("arbitrary",),
                                             vmem_limit_bytes=VMEM_LIMIT),
        name="prompt_mixer",
    )(x, *weights)


def _ffn_kernel(x_ref, xs_ref, st_ref, g2_ref, fcw_ref, fcb_ref, gf_ref, wg_hbm, wu_hbm, wd_hbm,
                xo_ref, ffn_o_ref, xs_o_ref, st_o_ref,
                g_prev, wg_ref, wu_ref, wd_ref, *out_scratch, layer, last_layer):
    t = pl.program_id(0)
    n_tiles = pl.num_programs(0) - 1

    @pl.when(t == 0)
    def _():
        g_prev[...] = jnp.zeros_like(g_prev)
        _load_weight_as_bf16(wg_hbm, layer, wg_ref, 256)
        _load_weight_as_bf16(wu_hbm, layer, wu_ref, 256)
        _load_weight_as_bf16(wd_hbm, layer, wd_ref, 768)

    @pl.when(t < n_tiles)
    def _():
        _prompt_ffn_tile(t, n_tiles, x_ref, g2_ref, fcw_ref, fcb_ref, gf_ref, wg_ref, wu_ref, wd_ref,
                         xo_ref, ffn_o_ref, g_prev, out_scratch, last_layer)

    @pl.when(t == n_tiles)
    def _():
        _sample_ffn_rows(xs_ref, st_ref, g2_ref, fcw_ref, fcb_ref, gf_ref, wg_ref, wu_ref, wd_ref,
                         xs_o_ref, st_o_ref, last_layer)


def _prompt_ffn_tile(t, n_steps, x_ref, g2_ref, fcw_ref, fcb_ref, gf_ref, wg_ref, wu_ref, wd_ref,
                     xo_ref, ffn_o_ref, g_prev, out_scratch, last_layer):
    x = x_ref[...]
    xn = _bf(_rms(x, g2_ref[...]))
    acc = x
    npre = (FFN_CONV - 1) * PB
    n_chunks = D_FF // FFN_C
    col = lambda c: slice(c * FFN_C, (c + 1) * FFN_C)
    up_proj = lambda c: (_dot(xn, wg_ref[:, col(c)]), _dot(xn, wu_ref[:, col(c)]))
    nxt = up_proj(0)
    for c in range(n_chunks):
        cs = col(c)
        g_pre, up = nxt
        if c + 1 < n_chunks:
            nxt = up_proj(c + 1)
        g_ext = jnp.concatenate([g_prev[:, cs], g_pre], axis=0)
        y = fcb_ref[:, cs] + fcw_ref[FFN_CONV - 1:FFN_CONV, cs] * g_pre
        for m in range(1, FFN_CONV):
            lo = npre - m * PB
            y = y + fcw_ref[FFN_CONV - 1 - m:FFN_CONV - m, cs] * g_ext[lo:lo + FFN_R]
        g_prev[:, cs] = g_ext[FFN_R:]
        h = _gelu(y) * up
        acc = acc + _dot(_bf(h), wd_ref[cs, :])
    if last_layer:
        ybuf, y_sem = out_scratch
        slot = t % 2
        out_copies = functools.partial(_prompt_tile_out_copies, ybuf, xo_ref, y_sem)

        @pl.when(t >= 2)
        def _():
            for c in out_copies(t - 2, slot):
                c.wait()

        ybuf[slot] = _rms(acc, gf_ref[...]).reshape(FFN_T, PB, D_MODEL)
        for c in out_copies(t, slot):
            c.start()

        @pl.when(t == n_steps - 1)
        def _():
            @pl.when(t >= 1)
            def _():
                for c in out_copies(t - 1, 1 - slot):
                    c.wait()
            for c in out_copies(t, slot):
                c.wait()
    else:
        xo_ref[...] = acc
    ffn_o_ref[...] = g_prev[...]


def _layer_ffn(x_tm, xs, st, w, l, last_layer):
    n_rows = x_tm.shape[0]
    n_tiles = n_rows // FFN_R
    n = xs.shape[0]
    f32 = jnp.float32
    row_spec = pl.BlockSpec((FFN_R, D_MODEL), lambda t: (jnp.minimum(t, n_tiles - 1), 0))
    pre_rows = (FFN_CONV - 1) * PB
    out_scratch = []
    if last_layer:
        out_spec = _ANY
        out_shape = jax.ShapeDtypeStruct((PB, n_rows // PB, D_MODEL), f32)
        out_scratch = [pltpu.VMEM((2, FFN_T, PB, D_MODEL), f32), pltpu.SemaphoreType.DMA((2, PB))]
    else:
        out_spec, out_shape = row_spec, jax.ShapeDtypeStruct((n_rows, D_MODEL), f32)
    small = [w[k] for k in _FFN_SMALL]
    big = [w[k] for k in _FFN_BIG]
    sample_out = [jax.ShapeDtypeStruct((n, D_MODEL), f32),
                  jax.ShapeDtypeStruct((FFN_CONV - 1, n, D_FF), f32)]
    return pl.pallas_call(
        functools.partial(_ffn_kernel, layer=l, last_layer=last_layer),
        grid=(n_tiles + 1,),
        in_specs=[row_spec, _whole_spec(xs), _layer_spec(st, l)] + [_layer_spec(a, l) for a in small]
                 + [_whole_spec(w['gf'])] + [_ANY] * len(big),
        out_specs=[out_spec, pl.BlockSpec((pre_rows, D_FF), lambda t: (0, 0))]
                  + [pl.BlockSpec(s.shape, lambda t, nd=len(s.shape): (0,) * nd) for s in sample_out],
        out_shape=[out_shape, jax.ShapeDtypeStruct((pre_rows, D_FF), f32)] + sample_out,
        scratch_shapes=[pltpu.VMEM((pre_rows, D_FF), f32)]
                       + [pltpu.VMEM(a.shape[1:], jnp.bfloat16) for a in big] + out_scratch,
        compiler_params=pltpu.CompilerParams(dimension_semantics=("arbitrary",),
                                             vmem_limit_bytes=VMEM_LIMIT),
        name="layer_ffn",
    )(x_tm, xs, st, *small, w['gf'], *big)


def _sample_mixer_kernel(x_ref, pool_st_ref, rconv_st_ref, h0_ref,
                         g1_ref, pool_w_ref, pool_scale_ref, cw_ref, cb_ref,
                         lru_w_ref, lru_b_ref, lam_ref, vg_ref, ws0_ref, bs0_ref,
                         w_in_ref, w_pa_ref, w_pb_ref, w_pc_ref, w_o_ref,
                         xo_ref, pool_o_ref, rconv_o_ref, h_o_ref, v_o_ref):
    x = x_ref[...]
    xn = _bf(_rms(x, g1_ref[...]))

    a_in = _dot(xn, w_in_ref[:, C_POOL:C_POOL + D_POOL])
    d_parts = []
    for g, w in enumerate(POOL_WINDOWS):
        s = a_in[:, g * POOL_GW:(g + 1) * POOL_GW]
        for m in range(1, w):
            s = s + pool_st_ref[POOL_STATE - m, :, g * POOL_GW:(g + 1) * POOL_GW]
        cnt = float(min(PAST_LEN + 1, w))
        d_parts.append(s / cnt - a_in[:, g * POOL_GW:(g + 1) * POOL_GW])
    pool_o_ref[:POOL_STATE - 1] = pool_st_ref[1:]
    pool_o_ref[POOL_STATE - 1] = a_in
    pa = _pool_project(d_parts, pool_w_ref, pool_scale_ref, w_pa_ref)

    b_x = _dot(xn, w_in_ref[:, C_BX:C_BX + D_RNN])
    bc = cb_ref[...] + cw_ref[CONV_WIDTH - 1:CONV_WIDTH, :] * b_x
    for m in range(CONV_WIDTH - 1):
        bc = bc + cw_ref[m:m + 1, :] * rconv_st_ref[m]
    rconv_o_ref[:CONV_WIDTH - 2] = rconv_st_ref[1:]
    rconv_o_ref[CONV_WIDTH - 2] = b_x
    a_dec, b_inp = _lru_coeffs(bc, lru_w_ref, lru_b_ref, _lru_log_decay_scale(lam_ref[...]))
    h = a_dec * h0_ref[...] + b_inp
    h_o_ref[...] = h
    b_gate = _dot(xn, w_in_ref[:, C_BG:C_BG + D_RNN])
    pb = _dot(_bf(_gelu(b_gate) * h), w_pb_ref[...])

    u = _gelu(_dot(xn, w_in_ref[:, C_CU:C_CU + D_CHUNK]))
    v = _rms(_gelu(_dot(xn, w_in_ref[:, C_CV:C_CV + D_CHUNK])), vg_ref[...])
    v_o_ref[...] = v
    mix = ws0_ref[...] * v + bs0_ref[...]
    pc = _dot(_bf(u * mix), w_pc_ref[...])

    merged = _sigmoid(_dot(xn, w_in_ref[:, C_GA:C_GA + D_MODEL])) * pa
    merged = merged + _sigmoid(_dot(xn, w_in_ref[:, C_GB:C_GB + D_MODEL])) * pb
    merged = merged + _sigmoid(_dot(xn, w_in_ref[:, C_GC:C_GC + D_MODEL])) * pc
    xo_ref[...] = x + _dot(_bf(merged), w_o_ref[...])


def _sample_mixer(xs, pool_st, rconv_st, h0, w, l):
    n = xs.shape[0]
    f32 = jnp.float32
    acts = [xs, pool_st, rconv_st, h0]
    weights = [w[k] for k in _MIXER_SMALL + ('ws0', 'bs0')] + [w[k + '_bf'] for k in _MIXER_BIG]
    out_shape = [jax.ShapeDtypeStruct((n, D_MODEL), f32),
                 jax.ShapeDtypeStruct((POOL_STATE, n, D_POOL), f32),
                 jax.ShapeDtypeStruct((CONV_WIDTH - 1, n, D_RNN), f32),
                 jax.ShapeDtypeStruct((n, D_RNN), f32),
                 jax.ShapeDtypeStruct((n, D_CHUNK), f32)]
    return pl.pallas_call(
        _sample_mixer_kernel,
        grid=(1,),
        in_specs=[_whole_spec(xs)] + [_layer_spec(a, l) for a in acts[1:] + weights],
        out_specs=[pl.BlockSpec(s.shape, lambda t, nd=len(s.shape): (0,) * nd) for s in out_shape],
        out_shape=out_shape,
        compiler_params=pltpu.CompilerParams(dimension_semantics=("arbitrary",),
                                             vmem_limit_bytes=VMEM_LIMIT),
        name="sample_mixer",
    )(*acts, *weights)


def _sample_ffn_rows(x_ref, st_ref, g2_ref, fcw_ref, fcb_ref, gf_ref, wg_ref, wu_ref, wd_ref,
                     xo_ref, st_o_ref, last_layer):
    x = x_ref[...]
    xn = _bf(_rms(x, g2_ref[...]))
    g_pre = _dot(xn, wg_ref[...])
    y = fcb_ref[...] + fcw_ref[FFN_CONV - 1:FFN_CONV, :] * g_pre
    for m in range(FFN_CONV - 1):
        y = y + fcw_ref[m:m + 1, :] * st_ref[m]
    st_o_ref[:FFN_CONV - 2] = st_ref[1:]
    st_o_ref[FFN_CONV - 2] = g_pre
    h = _gelu(y) * _dot(xn, wu_ref[...])
    out = x + _dot(_bf(h), wd_ref[...])
    if last_layer:
        out = _rms(out, gf_ref[...])
    xo_ref[...] = out


def _prepare_weights(p):
    bf16 = jnp.bfloat16
    depth = p['w_in'].shape[0]
    row = lambda a: a.reshape(depth, 1, -1)
    return {
        'g1': row(p['norm1_g']),
        'pool_w': p['pool_w'].astype(bf16), 'pool_scale': row(p['pool_scale']),
        'cw': p['rnn_conv_w'], 'cb': row(p['rnn_conv_b']),
        'lru_w': jnp.concatenate([p['lru_wa'], p['lru_wx']], axis=-1).astype(bf16),
        'lru_b': jnp.stack([p['lru_ba'], p['lru_bx']], axis=1),
        'lam': row(p['lru_lambda']), 'vg': row(p['chunk_vnorm_g']),
        'ws': p['chunk_ws'].astype(bf16),
        'bs_tm': jnp.repeat(jnp.swapaxes(p['chunk_bs'], 1, 2), PB, axis=1),
        'ws0': row(jnp.repeat(p['chunk_ws'][:, :, 0, 0], CHUNK_GW, axis=1)),
        'bs0': row(jnp.repeat(p['chunk_bs'][:, :, 0], CHUNK_GW, axis=1)),
        'w_in_bf': p['w_in'].astype(bf16), 'w_pa_bf': p['w_pa'].astype(bf16),
        'w_pb_bf': p['w_pb'].astype(bf16), 'w_pc_bf': p['w_pc'].astype(bf16),
        'w_o_bf': p['w_o'].astype(bf16),
        'g2': row(p['norm2_g']), 'wg': p['ffn_wg'], 'wu': p['ffn_wu'], 'fcw': p['ffn_conv_w'],
        'fcb': row(p['ffn_conv_b']), 'wd': p['ffn_wd'],
        'gf': p['final_norm_g'].reshape(1, -1),
    }


def _batch_major(rows_tm, n):
    return jnp.transpose(rows_tm.reshape(-1, PB, rows_tm.shape[-1])[-n:], (1, 0, 2))


def kernel(x_prompt, x_sample, state_pool, state_rnn_conv, state_rnn_h, state_ffn_conv, norm1_g, w_in, pool_w, pool_scale, rnn_conv_w, rnn_conv_b, lru_wa, lru_ba, lru_wx, lru_bx, lru_lambda, chunk_vnorm_g, chunk_ws, chunk_bs, w_pa, w_pb, w_pc, w_o, norm2_g, ffn_wg, ffn_wu, ffn_conv_w, ffn_conv_b, ffn_wd, final_norm_g):
    p = dict(norm1_g=norm1_g, w_in=w_in, pool_w=pool_w, pool_scale=pool_scale,
             rnn_conv_w=rnn_conv_w, rnn_conv_b=rnn_conv_b, lru_wa=lru_wa, lru_ba=lru_ba,
             lru_wx=lru_wx, lru_bx=lru_bx, lru_lambda=lru_lambda, chunk_vnorm_g=chunk_vnorm_g,
             chunk_ws=chunk_ws, chunk_bs=chunk_bs, w_pa=w_pa, w_pb=w_pb, w_pc=w_pc, w_o=w_o,
             norm2_g=norm2_g, ffn_wg=ffn_wg, ffn_wu=ffn_wu, ffn_conv_w=ffn_conv_w,
             ffn_conv_b=ffn_conv_b, ffn_wd=ffn_wd, final_norm_g=final_norm_g)
    depth = w_in.shape[0]
    bp, tp, _ = x_prompt.shape
    ns = x_sample.shape[0]
    assert bp == PB and tp % TILE_T == 0 and x_sample.shape[1] == 1
    w = _prepare_weights(p)

    xp = x_prompt
    xs = x_sample.reshape(ns, D_MODEL)
    pool_st = jnp.swapaxes(state_pool, 1, 2)
    rconv_st = jnp.swapaxes(state_rnn_conv, 1, 2)
    ffn_st = jnp.swapaxes(state_ffn_conv, 1, 2)
    outs = {k: [] for k in ('pool_p', 'pool_s', 'rc_p', 'rc_s', 'h_p', 'h_s', 'ff_p', 'ff_s', 'cv_s')}
    for l in range(depth):
        last = l == depth - 1
        xp, pool_tm, rconv_tm, h_p = _prompt_mixer(xp, w, l, batch_major_in=(l == 0))
        xs, pool_s, rconv_s, h_s, v_s = _sample_mixer(xs, pool_st, rconv_st, state_rnn_h, w, l)
        xp, ffn_tm, xs, ffn_s = _layer_ffn(xp, xs, ffn_st, w, l, last)
        outs['pool_p'].append(_batch_major(pool_tm, POOL_STATE))
        outs['rc_p'].append(_batch_major(rconv_tm, CONV_WIDTH - 1))
        outs['h_p'].append(h_p)
        outs['ff_p'].append(_batch_major(ffn_tm, FFN_CONV - 1))
        outs['pool_s'].append(jnp.swapaxes(pool_s, 0, 1))
        outs['rc_s'].append(jnp.swapaxes(rconv_s, 0, 1))
        outs['h_s'].append(h_s)
        outs['ff_s'].append(jnp.swapaxes(ffn_s, 0, 1))
        outs['cv_s'].append(v_s.reshape(ns, 1, D_CHUNK))
    st = jnp.stack
    return (xp, xs.reshape(ns, 1, D_MODEL), st(outs['pool_p']), st(outs['pool_s']),
            st(outs['rc_p']), st(outs['rc_s']), st(outs['h_p']), st(outs['h_s']),
            st(outs['ff_p']), st(outs['ff_s']), st(outs['cv_s']))
```

```python
import functools
import math

import jax
import jax.numpy as jnp
from jax import lax
from jax.experimental import pallas as pl
from jax.experimental.pallas import tpu as pltpu

D_MODEL = 1024
D_POOL = 512
POOL_WINDOWS = (2, 4, 8, 16)
POOL_GW = 128
POOL_STATE = 15
D_RNN = 1024
RNN_HEADS = 8
RNN_HD = 128
CONV_WIDTH = 4
LRU_C = 8.0
D_CHUNK = 512
CHUNK = 128
CHUNK_GROUPS = 4
CHUNK_GW = 128
D_FF = 3072
FFN_CONV = 3
EPS = 1e-6
PAST_LEN = 16384

C_POOL = 0
C_BX = D_POOL
C_BG = C_BX + D_RNN
C_CU = C_BG + D_RNN
C_CV = C_CU + D_CHUNK
C_GA = C_CV + D_CHUNK
C_GB = C_GA + D_MODEL
C_GC = C_GB + D_MODEL

PB = 8
PB_SHIFT = PB.bit_length() - 1
TILE_T = CHUNK
TILE_R = TILE_T * PB
SUB_T = 64
SUB_R = SUB_T * PB
FFN_R = 512
FFN_T = FFN_R // PB
FFN_C = 1536
VMEM_LIMIT = 60 * 1024 * 1024

_GELU_K = math.sqrt(2.0 / math.pi)


def _dot(a, b):
    return jnp.dot(a, b, preferred_element_type=jnp.float32)


def _bf(x):
    return x.astype(jnp.bfloat16)


def _gelu(x):
    return 0.5 * x * (1.0 + jnp.tanh(_GELU_K * (x + 0.044715 * (x * x * x))))


def _sigmoid(x):
    return 0.5 * (jnp.tanh(0.5 * x) + 1.0)


def _rms(x, g):
    return x * lax.rsqrt(jnp.mean(x * x, axis=-1, keepdims=True) + EPS) * g


def _lru_log_decay_scale(lam):
    z = -lam
    sp = jnp.maximum(z, 0.0) + jnp.log1p(jnp.exp(-jnp.abs(z)))
    return -LRU_C * sp


def _lru_coeffs(bc, lru_w_ref, lru_b_ref, dscale, after_head=None):
    a_parts, b_parts = [], []
    for h in range(RNN_HEADS):
        if after_head is not None and h > 0:
            after_head(h - 1)
        cs = slice(h * RNN_HD, (h + 1) * RNN_HD)
        bch = bc[:, cs]
        ri = _dot(_bf(bch), lru_w_ref[h])
        r = _sigmoid(ri[:, :RNN_HD] + lru_b_ref[0:1, cs])
        i = _sigmoid(ri[:, RNN_HD:] + lru_b_ref[1:2, cs])
        log_a = dscale[:, cs] * r
        a = jnp.exp(log_a)
        a_parts.append(a)
        b_parts.append(jnp.sqrt((1.0 - a) * (1.0 + a)) * (i * bch))
    return jnp.concatenate(a_parts, axis=1), jnp.concatenate(b_parts, axis=1)


def _pool_project(d_parts, pool_w_ref, pool_scale_ref, w_pa_ref):
    ya = [_dot(_bf(d), pool_w_ref[g]) * pool_scale_ref[:, g * POOL_GW:(g + 1) * POOL_GW]
          for g, d in enumerate(d_parts)]
    return _dot(_bf(jnp.concatenate(ya, axis=1)), w_pa_ref[...])


WEIGHT_STAGE_SLOTS = 4


def _load_weight_as_bf16(src_hbm, layer, dst_ref, rows_per_chunk):
    n_rows, n_cols = dst_ref.shape
    assert n_rows % rows_per_chunk == 0
    n_chunks = n_rows // rows_per_chunk
    n_slots = WEIGHT_STAGE_SLOTS

    def body(stage, sem):
        def copy(i):
            rows = pl.ds(i * rows_per_chunk, rows_per_chunk)
            return pltpu.make_async_copy(src_hbm.at[layer, rows, :], stage.at[i % n_slots],
                                         sem.at[i % n_slots])

        for i in range(min(n_slots - 1, n_chunks)):
            copy(i).start()
        for i in range(n_chunks):
            if i + n_slots - 1 < n_chunks:
                copy(i + n_slots - 1).start()
            copy(i).wait()
            dst_ref[i * rows_per_chunk:(i + 1) * rows_per_chunk, :] = _bf(stage[i % n_slots])

    pl.run_scoped(body, pltpu.VMEM((n_slots, rows_per_chunk, n_cols), jnp.float32),
                  pltpu.SemaphoreType.DMA((n_slots,)))


def _prompt_tile_in_copies(x_hbm, xbuf, sem, tile, slot):
    return [pltpu.make_async_copy(x_hbm.at[s, pl.ds(tile * TILE_T, TILE_T), :],
                                  xbuf.at[slot, :, s, :], sem.at[slot, s]) for s in range(PB)]


def _prompt_tile_out_copies(ybuf, y_hbm, sem, tile, slot):
    return [pltpu.make_async_copy(ybuf.at[slot, :, s, :],
                                  y_hbm.at[s, pl.ds(tile * FFN_T, FFN_T), :], sem.at[slot, s])
            for s in range(PB)]


def _build_time_major_spatial_weights(ws_ref, kron_refs):
    row_t = lax.shift_right_logical(lax.broadcasted_iota(jnp.int32, (SUB_R, SUB_T), 0), PB_SHIFT)
    repeat_rows = _bf(row_t == lax.broadcasted_iota(jnp.int32, (SUB_R, SUB_T), 1))
    for k, kron_ref in enumerate(kron_refs):
        n_t, n_r = (k + 1) * SUB_T, (k + 1) * SUB_R
        col_t = lax.shift_right_logical(lax.broadcasted_iota(jnp.int32, (n_t, n_r), 1), PB_SHIFT)
        repeat_cols = _bf(col_t == lax.broadcasted_iota(jnp.int32, (n_t, n_r), 0))
        rr = lax.broadcasted_iota(jnp.int32, (SUB_R, n_r), 0)
        cc = lax.broadcasted_iota(jnp.int32, (SUB_R, n_r), 1)
        same_seq = (rr & (PB - 1)) == (cc & (PB - 1))
        causal = (lax.shift_right_logical(rr, PB_SHIFT) + k * SUB_T) >= lax.shift_right_logical(cc, PB_SHIFT)
        keep = same_seq & causal
        for g in range(CHUNK_GROUPS):
            rows = _dot(repeat_rows, ws_ref[g, k * SUB_T:(k + 1) * SUB_T, 0:n_t])
            full = _dot(_bf(rows), repeat_cols)
            kron_ref[g] = _bf(jnp.where(keep, full, 0.0))


def _prompt_mixer_kernel(x_ref, g1_ref, pool_w_ref, pool_scale_ref, cw_ref, cb_ref,
                         lru_w_ref, lru_b_ref, lam_ref, vg_ref, ws_ref, bs_ref,
                         w_in_ref, w_pa_ref, w_pb_ref, w_pc_ref, w_o_ref,
                         xo_ref, pool_o_ref, rconv_o_ref, h_o_ref,
                         pool_prev, conv_prev, h_carry, *scratch, batch_major_in):
    t = pl.program_id(0)
    n_sub = TILE_T // SUB_T
    kron_refs = scratch[:n_sub]

    @pl.when(t == 0)
    def _():
        pool_prev[...] = jnp.zeros_like(pool_prev)
        conv_prev[...] = jnp.zeros_like(conv_prev)
        h_carry[...] = jnp.zeros_like(h_carry)
        _build_time_major_spatial_weights(ws_ref, kron_refs)

    if batch_major_in:
        xbuf, x_sem = scratch[n_sub:]
        slot = t % 2

        @pl.when(t == 0)
        def _():
            for c in _prompt_tile_in_copies(x_ref, xbuf, x_sem, 0, 0):
                c.start()

        @pl.when(t + 1 < pl.num_programs(0))
        def _():
            for c in _prompt_tile_in_copies(x_ref, xbuf, x_sem, t + 1, 1 - slot):
                c.start()

        for c in _prompt_tile_in_copies(x_ref, xbuf, x_sem, t, slot):
            c.wait()

    dscale = _lru_log_decay_scale(lam_ref[...])
    g1 = g1_ref[...]
    pool_pre = pool_prev[...]
    conv_pre = conv_prev[...]
    h = h_carry[...]
    npre = pool_pre.shape[0]
    v_parts = []

    def load_normed(k):
        if batch_major_in:
            x = xbuf[slot, k * SUB_T:(k + 1) * SUB_T].reshape(SUB_R, D_MODEL)
        else:
            x = x_ref[k * SUB_R:(k + 1) * SUB_R, :]
        return x, _bf(_rms(x, g1))

    x, xn = load_normed(0)
    a_in = _dot(xn, w_in_ref[:, C_POOL:C_POOL + D_POOL])
    b_x = _dot(xn, w_in_ref[:, C_BX:C_BX + D_RNN])
    for k in range(n_sub):
        r0 = k * SUB_R
        proj = lambda c0, n, xn=xn: _dot(xn, w_in_ref[:, c0:c0 + n])

        a_ext = jnp.concatenate([pool_pre, a_in], axis=0)
        pos = t * TILE_T + k * SUB_T + lax.shift_right_logical(
            lax.broadcasted_iota(jnp.int32, (SUB_R, POOL_GW), 0), PB_SHIFT)
        d_parts = []
        for g, w in enumerate(POOL_WINDOWS):
            cs = slice(g * POOL_GW, (g + 1) * POOL_GW)
            s = a_ext[npre - (w - 1) * PB:, cs]
            sh = 1
            while sh < w:
                s = s[sh * PB:] + s[:-sh * PB]
                sh *= 2
            cnt = jnp.minimum(pos + 1, w).astype(jnp.float32)
            d_parts.append(s / cnt - a_in[:, cs])
        pool_pre = a_ext[SUB_R:]
        pa = _pool_project(d_parts, pool_w_ref, pool_scale_ref, w_pa_ref)

        b_ext = jnp.concatenate([conv_pre, b_x], axis=0)
        bc = cb_ref[...] + cw_ref[CONV_WIDTH - 1:CONV_WIDTH, :] * b_x
        for m in range(1, CONV_WIDTH):
            lo = (CONV_WIDTH - 1 - m) * PB
            bc = bc + cw_ref[CONV_WIDTH - 1 - m:CONV_WIDTH - m, :] * b_ext[lo:lo + SUB_R]
        conv_pre = b_ext[SUB_R:]
        late = {}
        late_cols = {0: ('bg', C_BG, D_RNN), 1: ('cu', C_CU, D_CHUNK), 2: ('cv', C_CV, D_CHUNK),
                     3: ('ga', C_GA, D_MODEL), 4: ('gb', C_GB, D_MODEL), 5: ('gc', C_GC, D_MODEL)}

        def issue_late(hd):
            if hd in late_cols:
                name, c0, n = late_cols[hd]
                late[name] = proj(c0, n)

        a_dec, b_inp = _lru_coeffs(bc, lru_w_ref, lru_b_ref, dscale, after_head=issue_late)
        hs = []
        for j in range(SUB_T):
            h = a_dec[j * PB:(j + 1) * PB] * h + b_inp[j * PB:(j + 1) * PB]
            hs.append(h)
        pb = _dot(_bf(_gelu(late['bg']) * jnp.concatenate(hs, axis=0)), w_pb_ref[...])

        u = _gelu(late['cu'])
        v = _rms(_gelu(late['cv']), vg_ref[...])
        v_parts.append(_bf(v))
        v_seen = jnp.concatenate(v_parts, axis=0)
        mix = []
        for g in range(CHUNK_GROUPS):
            cs = slice(g * CHUNK_GW, (g + 1) * CHUNK_GW)
            m = _dot(kron_refs[k][g], v_seen[:, cs])
            mix.append(m + bs_ref[r0:r0 + SUB_R, g:g + 1])
        pc = _dot(_bf(u * jnp.concatenate(mix, axis=1)), w_pc_ref[...])

        more = k + 1 < n_sub
        if more:
            x_nxt, xn_nxt = load_normed(k + 1)
        merged = _sigmoid(late['ga']) * pa
        if more:
            a_nxt = _dot(xn_nxt, w_in_ref[:, C_POOL:C_POOL + D_POOL])
        merged = merged + _sigmoid(late['gb']) * pb
        if more:
            b_nxt = _dot(xn_nxt, w_in_ref[:, C_BX:C_BX + D_RNN])
        merged = merged + _sigmoid(late['gc']) * pc
        xo_ref[r0:r0 + SUB_R, :] = x + _dot(_bf(merged), w_o_ref[...])
        if more:
            x, xn, a_in, b_x = x_nxt, xn_nxt, a_nxt, b_nxt

    pool_prev[...] = pool_pre
    conv_prev[...] = conv_pre
    h_carry[...] = h
    pool_o_ref[...] = pool_pre
    rconv_o_ref[...] = conv_pre
    h_o_ref[...] = h


def _layer_spec(a, l):
    nd = a.ndim
    return pl.BlockSpec((None,) + a.shape[1:], lambda *_: (l,) + (0,) * (nd - 1),
                        pipeline_mode=pl.Buffered(1))


def _whole_spec(a):
    nd = a.ndim
    return pl.BlockSpec(a.shape, lambda *_: (0,) * nd, pipeline_mode=pl.Buffered(1))


_MIXER_SMALL = ('g1', 'pool_w', 'pool_scale', 'cw', 'cb', 'lru_w', 'lru_b', 'lam', 'vg')
_MIXER_BIG = ('w_in', 'w_pa', 'w_pb', 'w_pc', 'w_o')
_FFN_SMALL = ('g2', 'fcw', 'fcb')
_FFN_BIG = ('wg', 'wu', 'wd')
_ANY = pl.BlockSpec(memory_space=pl.ANY)


def _prompt_mixer(x, w, l, batch_major_in):
    f32, bf16 = jnp.float32, jnp.bfloat16
    scratch_in = []
    if batch_major_in:
        n_rows = x.shape[0] * x.shape[1]
        x_spec = _ANY
        scratch_in = [pltpu.VMEM((2, TILE_T, PB, D_MODEL), f32), pltpu.SemaphoreType.DMA((2, PB))]
    else:
        n_rows = x.shape[0]
        x_spec = pl.BlockSpec((TILE_R, D_MODEL), lambda t: (t, 0))
    weights = [w[k] for k in _MIXER_SMALL + ('ws', 'bs_tm')] + [w[k + '_bf'] for k in _MIXER_BIG]
    pool_rows = (POOL_STATE + 1) * PB
    conv_rows = (CONV_WIDTH - 1) * PB
    return pl.pallas_call(
        functools.partial(_prompt_mixer_kernel, batch_major_in=batch_major_in),
        grid=(n_rows // TILE_R,),
        in_specs=[x_spec] + [_layer_spec(a, l) for a in weights],
        out_specs=[pl.BlockSpec((TILE_R, D_MODEL), lambda t: (t, 0)),
                   pl.BlockSpec((pool_rows, D_POOL), lambda t: (0, 0)),
                   pl.BlockSpec((conv_rows, D_RNN), lambda t: (0, 0)),
                   pl.BlockSpec((PB, D_RNN), lambda t: (0, 0))],
        out_shape=[jax.ShapeDtypeStruct((n_rows, D_MODEL), f32),
                   jax.ShapeDtypeStruct((pool_rows, D_POOL), f32),
                   jax.ShapeDtypeStruct((conv_rows, D_RNN), f32),
                   jax.ShapeDtypeStruct((PB, D_RNN), f32)],
        scratch_shapes=[pltpu.VMEM((pool_rows, D_POOL), f32),
                        pltpu.VMEM((conv_rows, D_RNN), f32),
                        pltpu.VMEM((PB, D_RNN), f32)]
                       + [pltpu.VMEM((CHUNK_GROUPS, SUB_R, (k + 1) * SUB_R), bf16)
                          for k in range(TILE_T // SUB_T)]
                       + scratch_in,
        compiler_params=pltpu.CompilerParams(dimension_semantics=("arbitrary",),
                                             vmem_limit_bytes=VMEM_LIMIT),
        name="prompt_mixer",
    )(x, *weights)


def _ffn_kernel(x_ref, xs_ref, st_ref, g2_ref, fcw_ref, fcb_ref, gf_ref, wg_hbm, wu_hbm, wd_hbm,
                xo_ref, ffn_o_ref, xs_o_ref, st_o_ref,
                g_prev, wg_ref, wu_ref, wd_ref, *out_scratch, layer, last_layer):
    t = pl.program_id(0)
    n_tiles = pl.num_programs(0) - 1

    @pl.when(t == 0)
    def _():
        g_prev[...] = jnp.zeros_like(g_prev)
        _load_weight_as_bf16(wg_hbm, layer, wg_ref, 256)
        _load_weight_as_bf16(wu_hbm, layer, wu_ref, 256)
        _load_weight_as_bf16(wd_hbm, layer, wd_ref, 768)

    @pl.when(t < n_tiles)
    def _():
        _prompt_ffn_tile(t, n_tiles, x_ref, g2_ref, fcw_ref, fcb_ref, gf_ref, wg_ref, wu_ref, wd_ref,
                         xo_ref, ffn_o_ref, g_prev, out_scratch, last_layer)

    @pl.when(t == n_tiles)
    def _():
        _sample_ffn_rows(xs_ref, st_ref, g2_ref, fcw_ref, fcb_ref, gf_ref, wg_ref, wu_ref, wd_ref,
                         xs_o_ref, st_o_ref, last_layer)


def _prompt_ffn_tile(t, n_steps, x_ref, g2_ref, fcw_ref, fcb_ref, gf_ref, wg_ref, wu_ref, wd_ref,
                     xo_ref, ffn_o_ref, g_prev, out_scratch, last_layer):
    x = x_ref[...]
    xn = _bf(_rms(x, g2_ref[...]))
    acc = x
    npre = (FFN_CONV - 1) * PB
    n_chunks = D_FF // FFN_C
    col = lambda c: slice(c * FFN_C, (c + 1) * FFN_C)
    up_proj = lambda c: (_dot(xn, wg_ref[:, col(c)]), _dot(xn, wu_ref[:, col(c)]))
    nxt = up_proj(0)
    for c in range(n_chunks):
        cs = col(c)
        g_pre, up = nxt
        if c + 1 < n_chunks:
            nxt = up_proj(c + 1)
        g_ext = jnp.concatenate([g_prev[:, cs], g_pre], axis=0)
        y = fcb_ref[:, cs] + fcw_ref[FFN_CONV - 1:FFN_CONV, cs] * g_pre
        for m in range(1, FFN_CONV):
            lo = npre - m * PB
            y = y + fcw_ref[FFN_CONV - 1 - m:FFN_CONV - m, cs] * g_ext[lo:lo + FFN_R]
        g_prev[:, cs] = g_ext[FFN_R:]
        h = _gelu(y) * up
        acc = acc + _dot(_bf(h), wd_ref[cs, :])
    if last_layer:
        ybuf, y_sem = out_scratch
        slot = t % 2
        out_copies = functools.partial(_prompt_tile_out_copies, ybuf, xo_ref, y_sem)

        @pl.when(t >= 2)
        def _():
            for c in out_copies(t - 2, slot):
                c.wait()

        ybuf[slot] = _rms(acc, gf_ref[...]).reshape(FFN_T, PB, D_MODEL)
        for c in out_copies(t, slot):
            c.start()

        @pl.when(t == n_steps - 1)
        def _():
            @pl.when(t >= 1)
            def _():
                for c in out_copies(t - 1, 1 - slot):
                    c.wait()
            for c in out_copies(t, slot):
                c.wait()
    else:
        xo_ref[...] = acc
    ffn_o_ref[...] = g_prev[...]


def _layer_ffn(x_tm, xs, st, w, l, last_layer):
    n_rows = x_tm.shape[0]
    n_tiles = n_rows // FFN_R
    n = xs.shape[0]
    f32 = jnp.float32
    row_spec = pl.BlockSpec((FFN_R, D_MODEL), lambda t: (jnp.minimum(t, n_tiles - 1), 0))
    pre_rows = (FFN_CONV - 1) * PB
    out_scratch = []
    if last_layer:
        out_spec = _ANY
        out_shape = jax.ShapeDtypeStruct((PB, n_rows // PB, D_MODEL), f32)
        out_scratch = [pltpu.VMEM((2, FFN_T, PB, D_MODEL), f32), pltpu.SemaphoreType.DMA((2, PB))]
    else:
        out_spec, out_shape = row_spec, jax.ShapeDtypeStruct((n_rows, D_MODEL), f32)
    small = [w[k] for k in _FFN_SMALL]
    big = [w[k] for k in _FFN_BIG]
    sample_out = [jax.ShapeDtypeStruct((n, D_MODEL), f32),
                  jax.ShapeDtypeStruct((FFN_CONV - 1, n, D_FF), f32)]
    return pl.pallas_call(
        functools.partial(_ffn_kernel, layer=l, last_layer=last_layer),
        grid=(n_tiles + 1,),
        in_specs=[row_spec, _whole_spec(xs), _layer_spec(st, l)] + [_layer_spec(a, l) for a in small]
                 + [_whole_spec(w['gf'])] + [_ANY] * len(big),
        out_specs=[out_spec, pl.BlockSpec((pre_rows, D_FF), lambda t: (0, 0))]
                  + [pl.BlockSpec(s.shape, lambda t, nd=len(s.shape): (0,) * nd) for s in sample_out],
        out_shape=[out_shape, jax.ShapeDtypeStruct((pre_rows, D_FF), f32)] + sample_out,
        scratch_shapes=[pltpu.VMEM((pre_rows, D_FF), f32)]
                       + [pltpu.VMEM(a.shape[1:], jnp.bfloat16) for a in big] + out_scratch,
        compiler_params=pltpu.CompilerParams(dimension_semantics=("arbitrary",),
                                             vmem_limit_bytes=VMEM_LIMIT),
        name="layer_ffn",
    )(x_tm, xs, st, *small, w['gf'], *big)


def _sample_mixer_kernel(x_ref, pool_st_ref, rconv_st_ref, h0_ref,
                         g1_ref, pool_w_ref, pool_scale_ref, cw_ref, cb_ref,
                         lru_w_ref, lru_b_ref, lam_ref, vg_ref, ws0_ref, bs0_ref,
                         w_in_ref, w_pa_ref, w_pb_ref, w_pc_ref, w_o_ref,
                         xo_ref, pool_o_ref, rconv_o_ref, h_o_ref, v_o_ref):
    x = x_ref[...]
    xn = _bf(_rms(x, g1_ref[...]))

    a_in = _dot(xn, w_in_ref[:, C_POOL:C_POOL + D_POOL])
    d_parts = []
    for g, w in enumerate(POOL_WINDOWS):
        s = a_in[:, g * POOL_GW:(g + 1) * POOL_GW]
        for m in range(1, w):
            s = s + pool_st_ref[POOL_STATE - m, :, g * POOL_GW:(g + 1) * POOL_GW]
        cnt = float(min(PAST_LEN + 1, w))
        d_parts.append(s / cnt - a_in[:, g * POOL_GW:(g + 1) * POOL_GW])
    pool_o_ref[:POOL_STATE - 1] = pool_st_ref[1:]
    pool_o_ref[POOL_STATE - 1] = a_in
    pa = _pool_project(d_parts, pool_w_ref, pool_scale_ref, w_pa_ref)

    b_x = _dot(xn, w_in_ref[:, C_BX:C_BX + D_RNN])
    bc = cb_ref[...] + cw_ref[CONV_WIDTH - 1:CONV_WIDTH, :] * b_x
    for m in range(CONV_WIDTH - 1):
        bc = bc + cw_ref[m:m + 1, :] * rconv_st_ref[m]
    rconv_o_ref[:CONV_WIDTH - 2] = rconv_st_ref[1:]
    rconv_o_ref[CONV_WIDTH - 2] = b_x
    a_dec, b_inp = _lru_coeffs(bc, lru_w_ref, lru_b_ref, _lru_log_decay_scale(lam_ref[...]))
    h = a_dec * h0_ref[...] + b_inp
    h_o_ref[...] = h
    b_gate = _dot(xn, w_in_ref[:, C_BG:C_BG + D_RNN])
    pb = _dot(_bf(_gelu(b_gate) * h), w_pb_ref[...])

    u = _gelu(_dot(xn, w_in_ref[:, C_CU:C_CU + D_CHUNK]))
    v = _rms(_gelu(_dot(xn, w_in_ref[:, C_CV:C_CV + D_CHUNK])), vg_ref[...])
    v_o_ref[...] = v
    mix = ws0_ref[...] * v + bs0_ref[...]
    pc = _dot(_bf(u * mix), w_pc_ref[...])

    merged = _sigmoid(_dot(xn, w_in_ref[:, C_GA:C_GA + D_MODEL])) * pa
    merged = merged + _sigmoid(_dot(xn, w_in_ref[:, C_GB:C_GB + D_MODEL])) * pb
    merged = merged + _sigmoid(_dot(xn, w_in_ref[:, C_GC:C_GC + D_MODEL])) * pc
    xo_ref[...] = x + _dot(_bf(merged), w_o_ref[...])


def _sample_mixer(xs, pool_st, rconv_st, h0, w, l):
    n = xs.shape[0]
    f32 = jnp.float32
    acts = [xs, pool_st, rconv_st, h0]
    weights = [w[k] for k in _MIXER_SMALL + ('ws0', 'bs0')] + [w[k + '_bf'] for k in _MIXER_BIG]
    out_shape = [jax.ShapeDtypeStruct((n, D_MODEL), f32),
                 jax.ShapeDtypeStruct((POOL_STATE, n, D_POOL), f32),
                 jax.ShapeDtypeStruct((CONV_WIDTH - 1, n, D_RNN), f32),
                 jax.ShapeDtypeStruct((n, D_RNN), f32),
                 jax.ShapeDtypeStruct((n, D_CHUNK), f32)]
    return pl.pallas_call(
        _sample_mixer_kernel,
        grid=(1,),
        in_specs=[_whole_spec(xs)] + [_layer_spec(a, l) for a in acts[1:] + weights],
        out_specs=[pl.BlockSpec(s.shape, lambda t, nd=len(s.shape): (0,) * nd) for s in out_shape],
        out_shape=out_shape,
        compiler_params=pltpu.CompilerParams(dimension_semantics=("arbitrary",),
                                             vmem_limit_bytes=VMEM_LIMIT),
        name="sample_mixer",
    )(*acts, *weights)


def _sample_ffn_rows(x_ref, st_ref, g2_ref, fcw_ref, fcb_ref, gf_ref, wg_ref, wu_ref, wd_ref,
                     xo_ref, st_o_ref, last_layer):
    x = x_ref[...]
    xn = _bf(_rms(x, g2_ref[...]))
    g_pre = _dot(xn, wg_ref[...])
    y = fcb_ref[...] + fcw_ref[FFN_CONV - 1:FFN_CONV, :] * g_pre
    for m in range(FFN_CONV - 1):
        y = y + fcw_ref[m:m + 1, :] * st_ref[m]
    st_o_ref[:FFN_CONV - 2] = st_ref[1:]
    st_o_ref[FFN_CONV - 2] = g_pre
    h = _gelu(y) * _dot(xn, wu_ref[...])
    out = x + _dot(_bf(h), wd_ref[...])
    if last_layer:
        out = _rms(out, gf_ref[...])
    xo_ref[...] = out


def _prepare_weights(p):
    bf16 = jnp.bfloat16
    depth = p['w_in'].shape[0]
    row = lambda a: a.reshape(depth, 1, -1)
    return {
        'g1': row(p['norm1_g']),
        'pool_w': p['pool_w'].astype(bf16), 'pool_scale': row(p['pool_scale']),
        'cw': p['rnn_conv_w'], 'cb': row(p['rnn_conv_b']),
        'lru_w': jnp.concatenate([p['lru_wa'], p['lru_wx']], axis=-1).astype(bf16),
        'lru_b': jnp.stack([p['lru_ba'], p['lru_bx']], axis=1),
        'lam': row(p['lru_lambda']), 'vg': row(p['chunk_vnorm_g']),
        'ws': p['chunk_ws'].astype(bf16),
        'bs_tm': jnp.repeat(jnp.swapaxes(p['chunk_bs'], 1, 2), PB, axis=1),
        'ws0': row(jnp.repeat(p['chunk_ws'][:, :, 0, 0], CHUNK_GW, axis=1)),
        'bs0': row(jnp.repeat(p['chunk_bs'][:, :, 0], CHUNK_GW, axis=1)),
        'w_in_bf': p['w_in'].astype(bf16), 'w_pa_bf': p['w_pa'].astype(bf16),
        'w_pb_bf': p['w_pb'].astype(bf16), 'w_pc_bf': p['w_pc'].astype(bf16),
        'w_o_bf': p['w_o'].astype(bf16),
        'g2': row(p['norm2_g']), 'wg': p['ffn_wg'], 'wu': p['ffn_wu'], 'fcw': p['ffn_conv_w'],
        'fcb': row(p['ffn_conv_b']), 'wd': p['ffn_wd'],
        'gf': p['final_norm_g'].reshape(1, -1),
    }


def _batch_major(rows_tm, n):
    return jnp.transpose(rows_tm.reshape(-1, PB, rows_tm.shape[-1])[-n:], (1, 0, 2))


def kernel(x_prompt, x_sample, state_pool, state_rnn_conv, state_rnn_h, state_ffn_conv, norm1_g, w_in, pool_w, pool_scale, rnn_conv_w, rnn_conv_b, lru_wa, lru_ba, lru_wx, lru_bx, lru_lambda, chunk_vnorm_g, chunk_ws, chunk_bs, w_pa, w_pb, w_pc, w_o, norm2_g, ffn_wg, ffn_wu, ffn_conv_w, ffn_conv_b, ffn_wd, final_norm_g):
    p = dict(norm1_g=norm1_g, w_in=w_in, pool_w=pool_w, pool_scale=pool_scale,
             rnn_conv_w=rnn_conv_w, rnn_conv_b=rnn_conv_b, lru_wa=lru_wa, lru_ba=lru_ba,
             lru_wx=lru_wx, lru_bx=lru_bx, lru_lambda=lru_lambda, chunk_vnorm_g=chunk_vnorm_g,
             chunk_ws=chunk_ws, chunk_bs=chunk_bs, w_pa=w_pa, w_pb=w_pb, w_pc=w_pc, w_o=w_o,
             norm2_g=norm2_g, ffn_wg=ffn_wg, ffn_wu=ffn_wu, ffn_conv_w=ffn_conv_w,
             ffn_conv_b=ffn_conv_b, ffn_wd=ffn_wd, final_norm_g=final_norm_g)
    depth = w_in.shape[0]
    bp, tp, _ = x_prompt.shape
    ns = x_sample.shape[0]
    assert bp == PB and tp % TILE_T == 0 and x_sample.shape[1] == 1
    w = _prepare_weights(p)

    xp = x_prompt
    xs = x_sample.reshape(ns, D_MODEL)
    pool_st = jnp.swapaxes(state_pool, 1, 2)
    rconv_st = jnp.swapaxes(state_rnn_conv, 1, 2)
    ffn_st = jnp.swapaxes(state_ffn_conv, 1, 2)
    outs = {k: [] for k in ('pool_p', 'pool_s', 'rc_p', 'rc_s', 'h_p', 'h_s', 'ff_p', 'ff_s', 'cv_s')}
    for l in range(depth):
        last = l == depth - 1
        xp, pool_tm, rconv_tm, h_p = _prompt_mixer(xp, w, l, batch_major_in=(l == 0))
        xs, pool_s, rconv_s, h_s, v_s = _sample_mixer(xs, pool_st, rconv_st, state_rnn_h, w, l)
        xp, ffn_tm, xs, ffn_s = _layer_ffn(xp, xs, ffn_st, w, l, last)
        outs['pool_p'].append(_batch_major(pool_tm, POOL_STATE))
        outs['rc_p'].append(_batch_major(rconv_tm, CONV_WIDTH - 1))
        outs['h_p'].append(h_p)
        outs['ff_p'].append(_batch_major(ffn_tm, FFN_CONV - 1))
        outs['pool_s'].append(jnp.swapaxes(pool_s, 0, 1))
        outs['rc_s'].append(jnp.swapaxes(rconv_s, 0, 1))
        outs['h_s'].append(h_s)
        outs['ff_s'].append(jnp.swapaxes(ffn_s, 0, 1))
        outs['cv_s'].append(v_s.reshape(ns, 1, D_CHUNK))
    st = jnp.stack
    return (xp, xs.reshape(ns, 1, D_MODEL), st(outs['pool_p']), st(outs['pool_s']),
            st(outs['rc_p']), st(outs['rc_s']), st(outs['h_p']), st(outs['h_s']),
            st(outs['ff_p']), st(outs['ff_s']), st(outs['cv_s']))
```
